```python
import jax, jax.numpy as jnp
from jax import lax
import numpy as np

D_MODEL = 4096
BATCH = 4
SEQ = 4096
DEPTH = 1

HEAD_DIM = 64
N_HEADS = D_MODEL // 128
N_KV_HEADS = N_HEADS // 8
GROUP = N_HEADS // N_KV_HEADS
WINDOW = 128
BLOCK = WINDOW
ROPE_THETA = 10000.0
CONV_DIM = D_MODEL // 2
CONV_WIDTH = 31
D_FF = 4 * D_MODEL
EPS = 1e-6

Q_DIM = N_HEADS * HEAD_DIM
KV_DIM = N_KV_HEADS * HEAD_DIM
OFF_Q = 0
OFF_K = OFF_Q + Q_DIM
OFF_V = OFF_K + KV_DIM
OFF_CONV = OFF_V + KV_DIM
OFF_GC = OFF_CONV + 2 * CONV_DIM
OFF_GA = OFF_GC + D_MODEL
IN_WIDTH = OFF_GA + D_MODEL

kernel_name = "hybrid_conformer_conv_swa_sink_gqa_block"


def rms_norm(x, g):
    xf = x.astype(jnp.float32)
    y = xf * lax.rsqrt(jnp.mean(xf * xf, axis=-1, keepdims=True) + EPS)
    return (y * g.astype(jnp.float32)).astype(x.dtype)


def layer_norm(x, g, b):
    xf = x.astype(jnp.float32)
    mu = jnp.mean(xf, axis=-1, keepdims=True)
    var = jnp.mean(jnp.square(xf - mu), axis=-1, keepdims=True)
    y = (xf - mu) * lax.rsqrt(var + EPS)
    return (y * g.astype(jnp.float32) + b.astype(jnp.float32)).astype(x.dtype)


def rope(t, pos):
    half = HEAD_DIM // 2
    inv_freq = ROPE_THETA ** (-jnp.arange(0, half, dtype=jnp.float32) / half)
    ang = pos.astype(jnp.float32)[:, None] * inv_freq[None, :]
    cos = jnp.cos(ang)[None, :, None, :]
    sin = jnp.sin(ang)[None, :, None, :]
    tf = t.astype(jnp.float32)
    t1, t2 = tf[..., :half], tf[..., half:]
    out = jnp.concatenate([t1 * cos - t2 * sin, t2 * cos + t1 * sin], axis=-1)
    return out.astype(t.dtype)


def conformer_conv(u, b_glu, w_dw, b_dw, ln_g, ln_b, w_conv_out):
    u = u + b_glu
    a, gate = u[..., :CONV_DIM], u[..., CONV_DIM:]
    glu = a * jax.nn.sigmoid(gate)
    dw = lax.conv_general_dilated(
        glu, w_dw[:, None, :].astype(glu.dtype), window_strides=(1,),
        padding=[(CONV_WIDTH - 1, 0)],
        dimension_numbers=("NWC", "WIO", "NWC"),
        feature_group_count=CONV_DIM) + b_dw
    z = jax.nn.silu(layer_norm(dw, ln_g, ln_b))
    return z @ w_conv_out


def swa_sink_attention(q, k, v, sinks, w_attn_out):
    b, s = q.shape[0], q.shape[1]
    nblk = s // BLOCK
    qb = q.reshape(b, nblk, BLOCK, N_KV_HEADS, GROUP, HEAD_DIM)
    kb = k.reshape(b, nblk, BLOCK, N_KV_HEADS, HEAD_DIM)
    vb = v.reshape(b, nblk, BLOCK, N_KV_HEADS, HEAD_DIM)
    pad = jnp.zeros_like(kb[:, :1])
    k2 = jnp.concatenate([jnp.concatenate([pad, kb[:, :-1]], axis=1), kb], axis=2)
    v2 = jnp.concatenate([jnp.concatenate([pad, vb[:, :-1]], axis=1), vb], axis=2)
    scale = HEAD_DIM ** -0.5
    scores = jnp.einsum("bnqhgd,bnkhd->bnhgqk", qb, k2).astype(jnp.float32) * scale
    blk = jnp.arange(nblk, dtype=jnp.int32)[:, None, None]
    qpos = blk * BLOCK + jnp.arange(BLOCK, dtype=jnp.int32)[None, :, None]
    kpos = (blk - 1) * BLOCK + jnp.arange(2 * BLOCK, dtype=jnp.int32)[None, None, :]
    diff = qpos - kpos
    valid = (diff >= 0) & (diff < WINDOW) & (kpos >= 0)
    scores = jnp.where(valid[None, :, None, None, :, :], scores, jnp.float32(-1e30))
    sink = jnp.broadcast_to(
        sinks.astype(jnp.float32).reshape(1, 1, N_KV_HEADS, GROUP, 1, 1),
        scores.shape[:-1] + (1,))
    probs = jax.nn.softmax(jnp.concatenate([scores, sink], axis=-1), axis=-1)[..., :-1]
    out = jnp.einsum("bnhgqk,bnkhd->bnqhgd", probs.astype(v.dtype), v2)
    return out.reshape(b, s, Q_DIM) @ w_attn_out


def setup_inputs(seed: int = 0) -> dict:
    key = jax.random.key(seed)
    ks = jax.random.split(key, 18)
    f32 = jnp.float32
    nrm = lambda k, shape, scale: jax.random.normal(k, shape, f32) * scale
    return {
        "x": nrm(ks[0], (BATCH, SEQ, D_MODEL), 1.0),
        "norm_mix_g": 1.0 + nrm(ks[1], (D_MODEL,), 0.02),
        "w_in": nrm(ks[2], (D_MODEL, IN_WIDTH), D_MODEL ** -0.5),
        "b_glu": nrm(ks[3], (2 * CONV_DIM,), 0.02),
        "w_dw": nrm(ks[4], (CONV_WIDTH, CONV_DIM), CONV_WIDTH ** -0.5),
        "b_dw": nrm(ks[5], (CONV_DIM,), 0.02),
        "conv_ln_g": 1.0 + nrm(ks[6], (CONV_DIM,), 0.02),
        "conv_ln_b": nrm(ks[7], (CONV_DIM,), 0.02),
        "w_conv_out": nrm(ks[8], (CONV_DIM, D_MODEL), CONV_DIM ** -0.5),
        "sinks": nrm(ks[9], (N_HEADS,), 0.5),
        "w_attn_out": nrm(ks[10], (Q_DIM, D_MODEL), Q_DIM ** -0.5),
        "w_out": nrm(ks[11], (D_MODEL, D_MODEL), D_MODEL ** -0.5),
        "norm_mlp_g": 1.0 + nrm(ks[12], (D_MODEL,), 0.02),
        "w_mlp_up": nrm(ks[13], (D_MODEL, D_FF), D_MODEL ** -0.5),
        "w_mlp_down": nrm(ks[14], (D_FF, D_MODEL), D_FF ** -0.5),
        "norm_final_g": 1.0 + nrm(ks[15], (D_MODEL,), 0.02),
    }


def reference(x, norm_mix_g, w_in, b_glu, w_dw, b_dw, conv_ln_g, conv_ln_b, w_conv_out,
              sinks, w_attn_out, w_out, norm_mlp_g, w_mlp_up, w_mlp_down, norm_final_g):
    b, s, _ = x.shape
    pos = jnp.arange(s, dtype=jnp.int32)
    for _layer in range(DEPTH):
        h = rms_norm(x, norm_mix_g)
        p = h @ w_in
        q = rope(p[..., OFF_Q:OFF_K].reshape(b, s, N_HEADS, HEAD_DIM), pos)
        k = rope(p[..., OFF_K:OFF_V].reshape(b, s, N_KV_HEADS, HEAD_DIM), pos)
        v = p[..., OFF_V:OFF_CONV].reshape(b, s, N_KV_HEADS, HEAD_DIM)
        conv_out = conformer_conv(p[..., OFF_CONV:OFF_GC], b_glu, w_dw, b_dw,
                                  conv_ln_g, conv_ln_b, w_conv_out)
        attn_out = swa_sink_attention(q, k, v, sinks, w_attn_out)
        merged = (jax.nn.sigmoid(p[..., OFF_GC:OFF_GA]) * conv_out
                  + jax.nn.sigmoid(p[..., OFF_GA:IN_WIDTH]) * attn_out)
        x = x + merged @ w_out
        h2 = rms_norm(x, norm_mlp_g)
        x = x + jnp.square(jax.nn.relu(h2 @ w_mlp_up)) @ w_mlp_down
    return rms_norm(x, norm_final_g)
```

```python
import functools

import jax
import jax.numpy as jnp
from jax import lax
from jax.experimental import pallas as pl
from jax.experimental.pallas import tpu as pltpu

EPS = 1e-6
HEAD_DIM = 64
GROUP = 8
WINDOW = 128
ROPE_THETA = 10000.0
MASK_VALUE = -1e30

V7X_LANES = 128
V7X_SUBLANES = 8
V7X_VMEM_BYTES = 64 * 1024 * 1024
VMEM_LIMIT_CAP = V7X_VMEM_BYTES - 6 * 1024 * 1024

F32 = jnp.float32
BF16 = jnp.bfloat16


def _tile(dim, pref):
    if dim <= pref:
        return dim
    t = (pref // V7X_LANES) * V7X_LANES
    while t > V7X_LANES and dim % t:
        t -= V7X_LANES
    assert dim % t == 0, (dim, pref)
    return t


def _nbytes(shape, dtype):
    n = jnp.dtype(dtype).itemsize
    for s in shape:
        n *= s
    return n


def _vmem_limit(block_bytes, scratch_bytes, temp_bytes):
    est = 2 * block_bytes + scratch_bytes + 2 * temp_bytes + (4 << 20)
    return int(min(max(est, 16 << 20), VMEM_LIMIT_CAP))


def _rmsnorm_kernel(x_ref, g_ref, o_ref):
    x = x_ref[...]
    ms = jnp.mean(x * x, axis=-1, keepdims=True)
    o_ref[...] = (x * lax.rsqrt(ms + EPS) * g_ref[...]).astype(o_ref.dtype)


def _rmsnorm(x, g, out_dtype):
    t, d = x.shape
    tr = _tile(t, 256)
    blocks = _nbytes((tr, d), F32) + _nbytes((tr, d), out_dtype)
    return pl.pallas_call(
        _rmsnorm_kernel,
        grid=(t // tr,),
        in_specs=[pl.BlockSpec((tr, d), lambda i: (i, 0)),
                  pl.BlockSpec((1, d), lambda i: (0, 0))],
        out_specs=pl.BlockSpec((tr, d), lambda i: (i, 0)),
        out_shape=jax.ShapeDtypeStruct((t, d), out_dtype),
        compiler_params=pltpu.CompilerParams(
            dimension_semantics=("parallel",),
            vmem_limit_bytes=_vmem_limit(blocks, 0, 2 * _nbytes((tr, d), F32))),
        name="rmsnorm",
    )(x, g.reshape(1, d))


def _fused_matmul_kernel(*refs, n_lhs, n_rhs, pairs, n_tile, n_col, epilogue):
    lhs = refs[:n_lhs]
    rhs = refs[n_lhs:n_lhs + n_rhs]
    tile = refs[n_lhs + n_rhs:n_lhs + n_rhs + n_tile]
    col = refs[n_lhs + n_rhs + n_tile:n_lhs + n_rhs + n_tile + n_col]
    o_ref = refs[-1]
    accs = [jnp.dot(lhs[a][...], rhs[b][...], preferred_element_type=F32) for a, b in pairs]
    out = epilogue(accs, [r[...] for r in tile], [r[...] for r in col])
    o_ref[...] = out.astype(o_ref.dtype)


def _fused_matmul(lhs, rhs, pairs, epilogue, out_dtype, *, tile_extras=(), col_extras=(),
                  tm=1024, tn=1024, name):
    m = lhs[0].shape[0]
    n = rhs[0].shape[1]
    tm = _tile(m, tm)
    tn = _tile(n, tn)
    in_specs, operands, blocks = [], [], 0
    for a in lhs:
        in_specs.append(pl.BlockSpec((tm, a.shape[1]), lambda i, j: (i, 0)))
        operands.append(a)
        blocks += _nbytes((tm, a.shape[1]), a.dtype)
    for b in rhs:
        in_specs.append(pl.BlockSpec((b.shape[0], tn), lambda i, j: (0, j)))
        operands.append(b)
        blocks += _nbytes((b.shape[0], tn), b.dtype)
    for arr, col_off in tile_extras:
        assert col_off % tn == 0
        in_specs.append(pl.BlockSpec((tm, tn), functools.partial(
            lambda i, j, o: (i, j + o), o=col_off // tn)))
        operands.append(arr)
        blocks += _nbytes((tm, tn), arr.dtype)
    for arr in col_extras:
        in_specs.append(pl.BlockSpec((1, tn), lambda i, j: (0, j)))
        operands.append(arr)
        blocks += _nbytes((V7X_SUBLANES, tn), arr.dtype)
    blocks += _nbytes((tm, tn), out_dtype)
    temps = (len(pairs) + 1) * _nbytes((tm, tn), F32)
    kernel = functools.partial(
        _fused_matmul_kernel, n_lhs=len(lhs), n_rhs=len(rhs), pairs=tuple(pairs),
        n_tile=len(tile_extras), n_col=len(col_extras), epilogue=epilogue)
    return pl.pallas_call(
        kernel,
        grid=(m // tm, n // tn),
        in_specs=in_specs,
        out_specs=pl.BlockSpec((tm, tn), lambda i, j: (i, j)),
        out_shape=jax.ShapeDtypeStruct((m, n), out_dtype),
        compiler_params=pltpu.CompilerParams(
            dimension_semantics=("parallel", "parallel"),
            vmem_limit_bytes=_vmem_limit(blocks, 0, temps)),
        name=name,
    )(*operands)


def _matmul_kgrid_kernel(a_ref, b_ref, r_ref, o_ref):
    k = pl.program_id(2)
    part = jnp.dot(a_ref[...], b_ref[...], preferred_element_type=F32)

    @pl.when(k == 0)
    def _():
        o_ref[...] = r_ref[...] + part

    @pl.when(k > 0)
    def _():
        o_ref[...] += part


def _matmul_kgrid(a, b, resid, *, tm=1024, tn=1024, tk=2048, name):
    m, kdim = a.shape
    n = b.shape[1]
    tm, tn, tk = _tile(m, tm), _tile(n, tn), _tile(kdim, tk)
    blocks = (_nbytes((tm, tk), a.dtype) + _nbytes((tk, tn), b.dtype)
              + 2 * _nbytes((tm, tn), F32))
    return pl.pallas_call(
        _matmul_kgrid_kernel,
        grid=(m // tm, n // tn, kdim // tk),
        in_specs=[pl.BlockSpec((tm, tk), lambda i, j, k: (i, k)),
                  pl.BlockSpec((tk, tn), lambda i, j, k: (k, j)),
                  pl.BlockSpec((tm, tn), lambda i, j, k: (i, j))],
        out_specs=pl.BlockSpec((tm, tn), lambda i, j, k: (i, j)),
        out_shape=jax.ShapeDtypeStruct((m, n), F32),
        compiler_params=pltpu.CompilerParams(
            dimension_semantics=("parallel", "parallel", "arbitrary"),
            vmem_limit_bytes=_vmem_limit(blocks, 0, 2 * _nbytes((tm, tn), F32))),
        name=name,
    )(a, b, resid)


def _rope_matmul_kernel(a_ref, b_ref, cos_ref, sin_lo_ref, sin_hi_ref, o_ref, *, n_rope_chunks):
    acc = jnp.dot(a_ref[...], b_ref[...], preferred_element_type=F32)
    half = HEAD_DIM // 2
    for c in range(acc.shape[1] // V7X_LANES):
        cs = slice(c * V7X_LANES, (c + 1) * V7X_LANES)
        xc = acc[:, cs]
        if c < n_rope_chunks:
            xc = (xc * cos_ref[...]
                  + pltpu.roll(xc, V7X_LANES - half, 1) * sin_lo_ref[...]
                  + pltpu.roll(xc, half, 1) * sin_hi_ref[...])
        o_ref[:, cs] = xc.astype(o_ref.dtype)


def _rope_matmul(a, b, tables, seq, n_rope_cols, *, tm=1024, tn=1024, name):
    m, kdim = a.shape
    n = b.shape[1]
    tm = _tile(seq, tm)
    tn = _tile(n, tn)
    assert n_rope_cols in (0, n) or tn == n
    seq_tiles = seq // tm
    tab_spec = pl.BlockSpec((tm, V7X_LANES), lambda i, j: (i % seq_tiles, 0))
    blocks = (_nbytes((tm, kdim), a.dtype) + _nbytes((kdim, tn), b.dtype)
              + 3 * _nbytes((tm, V7X_LANES), F32) + _nbytes((tm, tn), BF16))
    kernel = functools.partial(_rope_matmul_kernel,
                               n_rope_chunks=min(n_rope_cols, tn) // V7X_LANES)
    return pl.pallas_call(
        kernel,
        grid=(m // tm, n // tn),
        in_specs=[pl.BlockSpec((tm, kdim), lambda i, j: (i, 0)),
                  pl.BlockSpec((kdim, tn), lambda i, j: (0, j)),
                  tab_spec, tab_spec, tab_spec],
        out_specs=pl.BlockSpec((tm, tn), lambda i, j: (i, j)),
        out_shape=jax.ShapeDtypeStruct((m, n), BF16),
        compiler_params=pltpu.CompilerParams(
            dimension_semantics=("parallel", "parallel"),
            vmem_limit_bytes=_vmem_limit(blocks, 0, 2 * _nbytes((tm, tn), F32))),
        name=name,
    )(a, b, *tables)


def _rope_tables(seq, scale):
    half = HEAD_DIM // 2
    inv_freq = ROPE_THETA ** (-jnp.arange(0, half, dtype=F32) / half)
    ang = jnp.arange(seq, dtype=jnp.int32).astype(F32)[:, None] * inv_freq[None, :]
    cos = jnp.tile(jnp.cos(ang), (1, V7X_LANES // half)) * scale
    sin = jnp.tile(jnp.sin(ang), (1, V7X_LANES // half)) * scale
    first_half = (jnp.arange(V7X_LANES) % HEAD_DIM) < half
    return cos, jnp.where(first_half, -sin, 0.0), jnp.where(first_half, 0.0, sin)


def _conv_ln_silu_kernel(cur_ref, prev_ref, wdw_ref, bdw_ref, g_ref, b_ref, o_ref, win_ref, dw_ref,
                         *, tr, width, halo):
    i = pl.program_id(1)
    chans = cur_ref.shape[-1]
    win_ref[0:halo, :] = jnp.where(i > 0, prev_ref[0, tr - halo:tr, :], 0.0)
    win_ref[halo:halo + tr, :] = cur_ref[0]
    lead = halo - (width - 1)
    for c in range(chans // V7X_LANES):
        cs = slice(c * V7X_LANES, (c + 1) * V7X_LANES)
        acc = None
        for r in range(min(V7X_SUBLANES, width)):
            n_a = -(-(width - r) // V7X_SUBLANES)
            xr = win_ref[pl.ds(lead + r, tr + V7X_SUBLANES * (n_a - 1)), cs]
            for a in range(n_a):
                tap = V7X_SUBLANES * a + r
                term = xr[V7X_SUBLANES * a:V7X_SUBLANES * a + tr, :] * wdw_ref[tap:tap + 1, cs]
                acc = term if acc is None else acc + term
        dw_ref[:, cs] = acc + bdw_ref[:, cs]
    x = dw_ref[...]
    mu = jnp.mean(x, axis=-1, keepdims=True)
    xc = x - mu
    var = jnp.mean(xc * xc, axis=-1, keepdims=True)
    y = xc * lax.rsqrt(var + EPS) * g_ref[...] + b_ref[...]
    o_ref[0] = (y * jax.nn.sigmoid(y)).astype(o_ref.dtype)


def _conv_ln_silu(glu, w_dw, b_dw, ln_g, ln_b):
    bsz, seq, chans = glu.shape
    width = w_dw.shape[0]
    halo = -(-(width - 1) // V7X_SUBLANES) * V7X_SUBLANES
    tr = _tile(seq, 128)
    assert tr >= halo and chans % V7X_LANES == 0
    row = lambda v: v.reshape(1, chans)
    blocks = 2 * _nbytes((tr, chans), F32) + _nbytes((tr, chans), BF16) + 5 * _nbytes((32, chans), F32)
    scratch = _nbytes((tr + halo, chans), F32) + _nbytes((tr, chans), F32)
    kernel = functools.partial(_conv_ln_silu_kernel, tr=tr, width=width, halo=halo)
    return pl.pallas_call(
        kernel,
        grid=(bsz, seq // tr),
        in_specs=[pl.BlockSpec((1, tr, chans), lambda b, i: (b, i, 0)),
                  pl.BlockSpec((1, tr, chans), lambda b, i: (b, jnp.maximum(i - 1, 0), 0)),
                  pl.BlockSpec((width, chans), lambda b, i: (0, 0)),
                  pl.BlockSpec((1, chans), lambda b, i: (0, 0)),
                  pl.BlockSpec((1, chans), lambda b, i: (0, 0)),
                  pl.BlockSpec((1, chans), lambda b, i: (0, 0))],
        out_specs=pl.BlockSpec((1, tr, chans), lambda b, i: (b, i, 0)),
        out_shape=jax.ShapeDtypeStruct((bsz, seq, chans), BF16),
        scratch_shapes=[pltpu.VMEM((tr + halo, chans), F32), pltpu.VMEM((tr, chans), F32)],
        compiler_params=pltpu.CompilerParams(
            dimension_semantics=("parallel", "parallel"),
            vmem_limit_bytes=_vmem_limit(blocks, scratch, 4 * _nbytes((tr, chans), F32))),
        name="conv_ln_silu",
    )(glu, glu, w_dw, row(b_dw), row(ln_g), row(ln_b))


def _two_head_block_diag(slab, half):
    x = slab.astype(F32)
    lo = lax.broadcasted_iota(jnp.int32, x.shape, 1) < HEAD_DIM
    swapped = pltpu.roll(x, HEAD_DIM, 1)
    top = jnp.where(lo, x if half == 0 else swapped, 0.0)
    bot = jnp.where(lo, 0.0, swapped if half == 0 else x)
    return jnp.concatenate([top, bot], axis=0).astype(BF16)


def _attention_kernel(sink_ref, q_ref, kv_ref, o_ref, *, n_heads, n_kv):
    n = pl.program_id(1)
    kv_dim = n_kv * HEAD_DIM
    wstart = pl.multiple_of(jnp.maximum(n - 1, 0) * WINDOW, WINDOW)
    qi = lax.broadcasted_iota(jnp.int32, (WINDOW, 2 * WINDOW), 0)
    kj = lax.broadcasted_iota(jnp.int32, (WINDOW, 2 * WINDOW), 1)
    diff = qi - kj + (n * WINDOW - wstart)
    valid = (diff >= 0) & (diff < WINDOW)
    pairs_per_kv = (n_heads // n_kv) // 2
    for h in range(n_kv):
        slab, half = divmod(h, 2)
        ks = slice(slab * V7X_LANES, (slab + 1) * V7X_LANES)
        vs = slice(kv_dim + slab * V7X_LANES, kv_dim + (slab + 1) * V7X_LANES)
        kcat = _two_head_block_diag(kv_ref[0, pl.ds(wstart, 2 * WINDOW), ks], half)
        vcat = _two_head_block_diag(kv_ref[0, pl.ds(wstart, 2 * WINDOW), vs], half)
        for pair in range(h * pairs_per_kv, (h + 1) * pairs_per_kv):
            cs = slice(pair * V7X_LANES, (pair + 1) * V7X_LANES)
            s = lax.dot_general(q_ref[0, :, cs], kcat, (((1,), (1,)), ((), ())),
                                preferred_element_type=F32)
            probs = []
            for hh in range(2):
                sh = jnp.where(valid, s[:, hh * 2 * WINDOW:(hh + 1) * 2 * WINDOW], MASK_VALUE)
                sink = sink_ref[2 * pair + hh]
                mx = jnp.maximum(jnp.max(sh, axis=1, keepdims=True), sink)
                p = jnp.exp(sh - mx)
                den = jnp.sum(p, axis=1, keepdims=True) + jnp.exp(sink - mx)
                probs.append((p * (1.0 / den)).astype(BF16))
            out = jnp.dot(jnp.concatenate(probs, axis=1), vcat, preferred_element_type=F32)
            o_ref[0, :, cs] = out.astype(o_ref.dtype)


def _attention(q, kv, sinks, n_heads, n_kv):
    bsz, seq, q_dim = q.shape
    kv_cols = kv.shape[-1]
    assert seq % WINDOW == 0 and seq >= 2 * WINDOW and n_kv % 2 == 0 and (n_heads // n_kv) % 2 == 0
    blocks = 2 * _nbytes((WINDOW, q_dim), BF16) + _nbytes((seq, kv_cols), BF16)
    kernel = functools.partial(_attention_kernel, n_heads=n_heads, n_kv=n_kv)
    return pl.pallas_call(
        kernel,
        grid=(bsz, seq // WINDOW),
        in_specs=[pl.BlockSpec(memory_space=pltpu.SMEM),
                  pl.BlockSpec((1, WINDOW, q_dim), lambda b, i: (b, i, 0)),
                  pl.BlockSpec((1, seq, kv_cols), lambda b, i: (b, 0, 0))],
        out_specs=pl.BlockSpec((1, WINDOW, q_dim), lambda b, i: (b, i, 0)),
        out_shape=jax.ShapeDtypeStruct((bsz, seq, q_dim), BF16),
        compiler_params=pltpu.CompilerParams(
            dimension_semantics=("parallel", "parallel"),
            vmem_limit_bytes=_vmem_limit(blocks, 0, 8 << 20)),
        name="swa_attention",
    )(sinks, q, kv)


def _glu_epilogue(accs, tile, col):
    return (accs[0] + col[0]) * jax.nn.sigmoid(accs[1] + col[1])


def _sigmoid_epilogue(accs, tile, col):
    return jax.nn.sigmoid(accs[0])


def _gated_merge_epilogue(accs, tile, col):
    return tile[0] * accs[0] + tile[1] * accs[1]


def _residual_epilogue(accs, tile, col):
    return tile[0] + accs[0]


def _relu2_epilogue(accs, tile, col):
    return jnp.square(jnp.maximum(accs[0], 0.0))


def kernel(x, norm_mix_g, w_in, b_glu, w_dw, b_dw, conv_ln_g, conv_ln_b, w_conv_out, sinks,
           w_attn_out, w_out, norm_mlp_g, w_mlp_up, w_mlp_down, norm_final_g):
    bsz, seq, d = x.shape
    t = bsz * seq
    n_heads = sinks.shape[0]
    n_kv = n_heads // GROUP
    q_dim, kv_dim = n_heads * HEAD_DIM, n_kv * HEAD_DIM
    conv_dim = w_dw.shape[1]
    off_k, off_v, off_conv = q_dim, q_dim + kv_dim, q_dim + 2 * kv_dim
    off_gc = off_conv + 2 * conv_dim
    assert w_in.shape[1] == off_gc + 2 * d

    w_q = w_in[:, :off_k].astype(BF16)
    w_kv = w_in[:, off_k:off_conv].astype(BF16)
    w_a = w_in[:, off_conv:off_conv + conv_dim].astype(BF16)
    w_g = w_in[:, off_conv + conv_dim:off_gc].astype(BF16)
    w_gates = w_in[:, off_gc:].astype(BF16)
    w_co, w_ao, w_o = w_conv_out.astype(BF16), w_attn_out.astype(BF16), w_out.astype(BF16)
    w_up, w_down = w_mlp_up.astype(BF16), w_mlp_down.astype(BF16)

    xf = x.reshape(t, d)
    h = _rmsnorm(xf, norm_mix_g, BF16)

    q = _rope_matmul(h, w_q, _rope_tables(seq, HEAD_DIM ** -0.5), seq, q_dim, name="q_proj_rope")
    kv = _rope_matmul(h, w_kv, _rope_tables(seq, 1.0), seq, kv_dim, name="kv_proj_rope")
    glu = _fused_matmul([h], [w_a, w_g], [(0, 0), (0, 1)], _glu_epilogue, F32,
                        col_extras=[b_glu[:conv_dim].reshape(1, conv_dim),
                                    b_glu[conv_dim:].reshape(1, conv_dim)],
                        tn=512, name="conv_glu_proj")
    gates = _fused_matmul([h], [w_gates], [(0, 0)], _sigmoid_epilogue, F32, name="gate_proj")

    z = _conv_ln_silu(glu.reshape(bsz, seq, conv_dim), w_dw, b_dw, conv_ln_g, conv_ln_b)
    ao = _attention(q.reshape(bsz, seq, q_dim), kv.reshape(bsz, seq, 2 * kv_dim), sinks, n_heads, n_kv)

    merged = _fused_matmul([z.reshape(t, conv_dim), ao.reshape(t, q_dim)], [w_co, w_ao],
                           [(0, 0), (1, 1)], _gated_merge_epilogue, BF16,
                           tile_extras=[(gates, 0), (gates, d)], tn=512, name="branch_merge")
    x1 = _fused_matmul([merged], [w_o], [(0, 0)], _residual_epilogue, F32,
                       tile_extras=[(xf, 0)], tn=512, name="mix_out_proj")

    h2 = _rmsnorm(x1, norm_mlp_g, BF16)
    act = _fused_matmul([h2], [w_up], [(0, 0)], _relu2_epilogue, BF16, name="mlp_up")
    x2 = _matmul_kgrid(act, w_down, x1, name="mlp_down")
    y = _rmsnorm(x2, norm_final_g, F32)
    return y.reshape(bsz, seq, d)
```

```python
import functools

import jax
import jax.numpy as jnp
from jax import lax
from jax.experimental import pallas as pl
from jax.experimental.pallas import tpu as pltpu

EPS = 1e-6
HEAD_DIM = 64
GROUP = 8
WINDOW = 128
ROPE_THETA = 10000.0
MASK_VALUE = -1e30

V7X_LANES = 128
V7X_SUBLANES = 8
M_CHUNK = 256
V7X_VMEM_BYTES = 64 * 1024 * 1024
VMEM_LIMIT_CAP = V7X_VMEM_BYTES - 6 * 1024 * 1024

F32 = jnp.float32
BF16 = jnp.bfloat16


def _tile(dim, pref, offsets=()):
    if dim <= pref and all(o % dim == 0 for o in offsets):
        return dim
    t = (min(dim, pref) // V7X_LANES) * V7X_LANES
    while t > V7X_LANES and (dim % t or any(o % t for o in offsets)):
        t -= V7X_LANES
    assert dim % t == 0 and all(o % t == 0 for o in offsets), (dim, pref, offsets)
    return t


def _nbytes(shape, dtype):
    n = jnp.dtype(dtype).itemsize
    for s in shape:
        n *= s
    return n


def _vmem_limit(block_bytes, scratch_bytes, temp_bytes):
    est = 2 * block_bytes + scratch_bytes + 2 * temp_bytes + (4 << 20)
    return int(min(max(est, 16 << 20), VMEM_LIMIT_CAP))


def _rmsnorm_kernel(x_ref, g_ref, o_ref):
    x = x_ref[...]
    ms = jnp.mean(x * x, axis=-1, keepdims=True)
    o_ref[...] = (x * lax.rsqrt(ms + EPS) * g_ref[...]).astype(o_ref.dtype)


def _rmsnorm(x, g, out_dtype):
    t, d = x.shape
    tr = _tile(t, 256)
    blocks = _nbytes((tr, d), F32) + _nbytes((tr, d), out_dtype)
    return pl.pallas_call(
        _rmsnorm_kernel,
        grid=(t // tr,),
        in_specs=[pl.BlockSpec((tr, d), lambda i: (i, 0)),
                  pl.BlockSpec((1, d), lambda i: (0, 0))],
        out_specs=pl.BlockSpec((tr, d), lambda i: (i, 0)),
        out_shape=jax.ShapeDtypeStruct((t, d), out_dtype),
        compiler_params=pltpu.CompilerParams(
            dimension_semantics=("parallel",),
            vmem_limit_bytes=_vmem_limit(blocks, 0, 2 * _nbytes((tr, d), F32))),
        name="rmsnorm",
    )(x, g.reshape(1, d))


def _fused_matmul_kernel(*refs, n_lhs, n_rhs, pairs, n_tile, n_col, n_row, n_cast, epilogue):
    lhs = refs[:n_lhs]
    rhs = refs[n_lhs:n_lhs + n_rhs]
    n_in = n_lhs + n_rhs + n_tile + n_col + n_row
    extras = refs[n_lhs + n_rhs:n_in]
    cast_src = refs[n_in:n_in + n_cast]
    o_ref = refs[n_in + n_cast]
    cast_dst = refs[n_in + n_cast + 1:]
    for src, dst in zip(cast_src, cast_dst):
        dst[...] = src[...].astype(dst.dtype)
    tm = o_ref.shape[0]
    step = min(tm, M_CHUNK)
    for c in range(tm // step):
        rs = slice(c * step, (c + 1) * step)
        accs = [jnp.dot(lhs[a][rs, :], rhs[b][...], preferred_element_type=F32) for a, b in pairs]
        out = epilogue(accs, [r[rs, :] for r in extras[:n_tile]],
                       [r[...] for r in extras[n_tile:n_tile + n_col]],
                       [r[rs, :] for r in extras[n_tile + n_col:]])
        o_ref[rs, :] = out.astype(o_ref.dtype)


def _fused_matmul(lhs, rhs, pairs, epilogue, out_dtype, *, n, tile_extras=(), col_extras=(),
                  row_extras=(), side_casts=(), tm=1024, tn=1024, name):
    m = lhs[0].shape[0]
    tm = _tile(m, tm)
    tn = _tile(n, tn, [off for _, off in rhs] + [off for _, off in tile_extras])
    nj = n // tn
    n_steps = (m // tm) * nj
    in_specs, operands, blocks = [], [], 0
    for a in lhs:
        in_specs.append(pl.BlockSpec((tm, a.shape[1]), lambda i, j: (i, 0)))
        operands.append(a)
        blocks += _nbytes((tm, a.shape[1]), a.dtype)
    for b, col_off in rhs:
        assert col_off % tn == 0 and col_off + n <= b.shape[1]
        in_specs.append(pl.BlockSpec((b.shape[0], tn), functools.partial(
            lambda i, j, o: (0, j + o), o=col_off // tn)))
        operands.append(b)
        blocks += _nbytes((b.shape[0], tn), b.dtype)
    for arr, col_off in tile_extras:
        assert col_off % tn == 0
        in_specs.append(pl.BlockSpec((tm, tn), functools.partial(
            lambda i, j, o: (i, j + o), o=col_off // tn)))
        operands.append(arr)
        blocks += _nbytes((tm, tn), arr.dtype)
    for arr in col_extras:
        in_specs.append(pl.BlockSpec((1, tn), lambda i, j: (0, j)))
        operands.append(arr)
        blocks += _nbytes((V7X_SUBLANES, tn), arr.dtype)
    for arr in row_extras:
        in_specs.append(pl.BlockSpec((tm, arr.shape[1]), lambda i, j: (i, 0)))
        operands.append(arr)
        blocks += _nbytes((tm, arr.shape[1]), arr.dtype)
    out_specs = [pl.BlockSpec((tm, tn), lambda i, j: (i, j))]
    out_shape = [jax.ShapeDtypeStruct((m, n), out_dtype)]
    for arr in side_casts:
        rows, cols = arr.shape
        slab = rows // n_steps
        assert slab * n_steps == rows and slab % (2 * V7X_SUBLANES) == 0, (arr.shape, n_steps)
        spec = pl.BlockSpec((slab, cols), lambda i, j: (i * nj + j, 0))
        in_specs.append(spec)
        operands.append(arr)
        out_specs.append(spec)
        out_shape.append(jax.ShapeDtypeStruct(arr.shape, BF16))
        blocks += _nbytes((slab, cols), F32) + _nbytes((slab, cols), BF16)
    blocks += _nbytes((tm, tn), out_dtype)
    temps = (len(pairs) + 1) * _nbytes((tm, tn), F32)
    kernel = functools.partial(
        _fused_matmul_kernel, n_lhs=len(lhs), n_rhs=len(rhs), pairs=tuple(pairs),
        n_tile=len(tile_extras), n_col=len(col_extras), n_row=len(row_extras),
        n_cast=len(side_casts), epilogue=epilogue)
    outs = pl.pallas_call(
        kernel,
        grid=(m // tm, nj),
        in_specs=in_specs,
        out_specs=out_specs,
        out_shape=out_shape,
        compiler_params=pltpu.CompilerParams(
            dimension_semantics=("parallel", "parallel"),
            vmem_limit_bytes=_vmem_limit(blocks, 0, temps)),
        name=name,
    )(*operands)
    return (outs[0], list(outs[1:])) if side_casts else outs[0]


def _matmul_kgrid_kernel(a_ref, b_ref, r_ref, o_ref):
    @pl.when(pl.program_id(2) == 0)
    def _():
        o_ref[...] = r_ref[...]

    o_ref[...] += jnp.dot(a_ref[...], b_ref[...], preferred_element_type=F32)


def _matmul_kgrid(a, b, resid, *, tm=1024, tn=1024, tk=2048, name):
    m, kdim = a.shape
    n = b.shape[1]
    tm, tn, tk = _tile(m, tm), _tile(n, tn), _tile(kdim, tk)
    blocks = (_nbytes((tm, tk), a.dtype) + _nbytes((tk, tn), b.dtype)
              + 2 * _nbytes((tm, tn), F32))
    return pl.pallas_call(
        _matmul_kgrid_kernel,
        grid=(m // tm, n // tn, kdim // tk),
        in_specs=[pl.BlockSpec((tm, tk), lambda i, j, k: (i, k)),
                  pl.BlockSpec((tk, tn), lambda i, j, k: (k, j)),
                  pl.BlockSpec((tm, tn), lambda i, j, k: (i, j))],
        out_specs=pl.BlockSpec((tm, tn), lambda i, j, k: (i, j)),
        out_shape=jax.ShapeDtypeStruct((m, n), F32),
        compiler_params=pltpu.CompilerParams(
            dimension_semantics=("parallel", "parallel", "arbitrary"),
            vmem_limit_bytes=_vmem_limit(blocks, 0, 2 * _nbytes((tm, tn), F32))),
        name=name,
    )(a, b, resid)


def _residual_matmul_kernel(a_ref, b_ref, x_ref, g_ref, x1_ref, xg_ref, ssq_ref):
    @pl.when(pl.program_id(1) == 0)
    def _():
        ssq_ref[...] = jnp.zeros_like(ssq_ref)

    tm, tn = x1_ref.shape
    step = min(tm, M_CHUNK)
    for r in range(tm // step):
        rs = slice(r * step, (r + 1) * step)
        x1 = x_ref[rs, :] + jnp.dot(a_ref[rs, :], b_ref[...], preferred_element_type=F32)
        x1_ref[rs, :] = x1
        xg_ref[rs, :] = (x1 * g_ref[...]).astype(xg_ref.dtype)
        sq = x1 * x1
        part = sq[:, :V7X_LANES]
        for c in range(1, tn // V7X_LANES):
            part = part + sq[:, c * V7X_LANES:(c + 1) * V7X_LANES]
        ssq_ref[rs, :] += part


def _residual_matmul(a, b, x, g, *, tm=1024, tn=512, name):
    m, kdim = a.shape
    n = b.shape[1]
    tm, tn = _tile(m, tm), _tile(n, tn)
    blocks = (_nbytes((tm, kdim), a.dtype) + _nbytes((kdim, tn), b.dtype) + 2 * _nbytes((tm, tn), F32)
              + _nbytes((tm, tn), BF16) + _nbytes((tm, V7X_LANES), F32) + _nbytes((V7X_SUBLANES, tn), F32))
    return pl.pallas_call(
        _residual_matmul_kernel,
        grid=(m // tm, n // tn),
        in_specs=[pl.BlockSpec((tm, kdim), lambda i, j: (i, 0)),
                  pl.BlockSpec((kdim, tn), lambda i, j: (0, j)),
                  pl.BlockSpec((tm, tn), lambda i, j: (i, j)),
                  pl.BlockSpec((1, tn), lambda i, j: (0, j))],
        out_specs=[pl.BlockSpec((tm, tn), lambda i, j: (i, j)),
                   pl.BlockSpec((tm, tn), lambda i, j: (i, j)),
                   pl.BlockSpec((tm, V7X_LANES), lambda i, j: (i, 0))],
        out_shape=[jax.ShapeDtypeStruct((m, n), F32),
                   jax.ShapeDtypeStruct((m, n), BF16),
                   jax.ShapeDtypeStruct((m, V7X_LANES), F32)],
        compiler_params=pltpu.CompilerParams(
            dimension_semantics=("parallel", "arbitrary"),
            vmem_limit_bytes=_vmem_limit(blocks, 0, 3 * _nbytes((tm, tn), F32))),
        name=name,
    )(a, b, x, g.reshape(1, n))


def _rope_matmul_kernel(a_ref, b_ref, cos_ref, sin_lo_ref, sin_hi_ref, o_ref, *, n_rope_chunks):
    half = HEAD_DIM // 2
    tm, tn = o_ref.shape
    step = min(tm, M_CHUNK)
    for r in range(tm // step):
        rs = slice(r * step, (r + 1) * step)
        acc = jnp.dot(a_ref[rs, :], b_ref[...], preferred_element_type=F32)
        for c in range(tn // V7X_LANES):
            cs = slice(c * V7X_LANES, (c + 1) * V7X_LANES)
            xc = acc[:, cs]
            if c < n_rope_chunks:
                xc = (xc * cos_ref[rs, :]
                      + pltpu.roll(xc, V7X_LANES - half, 1) * sin_lo_ref[rs, :]
                      + pltpu.roll(xc, half, 1) * sin_hi_ref[rs, :])
            o_ref[rs, cs] = xc.astype(o_ref.dtype)


def _rope_matmul(a, b, col_off, n, tables, seq, n_rope_cols, *, tm=1024, tn=1024, name):
    m, kdim = a.shape
    tm = _tile(seq, tm)
    tn = _tile(n, tn, [col_off])
    assert n_rope_cols in (0, n) or tn == n
    assert col_off % tn == 0 and col_off + n <= b.shape[1]
    col_blk = col_off // tn
    seq_tiles = seq // tm
    tab_spec = pl.BlockSpec((tm, V7X_LANES), lambda i, j: (i % seq_tiles, 0))
    blocks = (_nbytes((tm, kdim), a.dtype) + _nbytes((kdim, tn), b.dtype)
              + 3 * _nbytes((tm, V7X_LANES), F32) + _nbytes((tm, tn), BF16))
    kernel = functools.partial(_rope_matmul_kernel,
                               n_rope_chunks=min(n_rope_cols, tn) // V7X_LANES)
    return pl.pallas_call(
        kernel,
        grid=(m // tm, n // tn),
        in_specs=[pl.BlockSpec((tm, kdim), lambda i, j: (i, 0)),
                  pl.BlockSpec((kdim, tn), lambda i, j: (0, j + col_blk)),
                  tab_spec, tab_spec, tab_spec],
        out_specs=pl.BlockSpec((tm, tn), lambda i, j: (i, j)),
        out_shape=jax.ShapeDtypeStruct((m, n), BF16),
        compiler_params=pltpu.CompilerParams(
            dimension_semantics=("parallel", "parallel"),
            vmem_limit_bytes=_vmem_limit(blocks, 0, 2 * _nbytes((tm, tn), F32))),
        name=name,
    )(a, b, *tables)


def _rope_tables(seq, scale):
    half = HEAD_DIM // 2
    inv_freq = ROPE_THETA ** (-jnp.arange(0, half, dtype=F32) / half)
    ang = jnp.arange(seq, dtype=jnp.int32).astype(F32)[:, None] * inv_freq[None, :]
    cos = jnp.tile(jnp.cos(ang), (1, V7X_LANES // half)) * scale
    sin = jnp.tile(jnp.sin(ang), (1, V7X_LANES // half)) * scale
    first_half = (jnp.arange(V7X_LANES) % HEAD_DIM) < half
    return cos, jnp.where(first_half, -sin, 0.0), jnp.where(first_half, 0.0, sin)


def _conv_taps_per_shift(width):
    return [-(-(width - r) // V7X_SUBLANES) for r in range(min(V7X_SUBLANES, width))]


def _conv_ln_silu_kernel(cur_ref, prev_ref, wdw_ref, bdw_ref, g_ref, b_ref, o_ref, win_ref, sh_ref, dw_ref,
                         *, tr, width, halo):
    i = pl.program_id(1)
    chans = cur_ref.shape[-1]
    win_ref[0:halo, :] = jnp.where(i > 0, prev_ref[0, tr - halo:tr, :], 0.0)
    win_ref[halo:halo + tr, :] = cur_ref[0]
    lead = halo - (width - 1)
    taps_at = _conv_taps_per_shift(width)
    for c in range(chans // V7X_LANES):
        cs = slice(c * V7X_LANES, (c + 1) * V7X_LANES)
        acc = None
        for r, n_a in enumerate(taps_at):
            rows = tr + V7X_SUBLANES * (n_a - 1)
            sh_ref[r, 0:rows, cs] = win_ref[pl.ds(lead + r, rows), cs]
            xr = sh_ref[r, 0:rows, cs]
            for a in range(n_a):
                tap = V7X_SUBLANES * a + r
                term = xr[V7X_SUBLANES * a:V7X_SUBLANES * a + tr, :] * wdw_ref[tap:tap + 1, cs]
                acc = term if acc is None else acc + term
        dw_ref[:, cs] = acc + bdw_ref[:, cs]
    x = dw_ref[...]
    mu = jnp.mean(x, axis=-1, keepdims=True)
    xc = x - mu
    var = jnp.mean(xc * xc, axis=-1, keepdims=True)
    y = xc * lax.rsqrt(var + EPS) * g_ref[...] + b_ref[...]
    o_ref[0] = (y * jax.nn.sigmoid(y)).astype(o_ref.dtype)


def _conv_ln_silu(glu, w_dw, b_dw, ln_g, ln_b):
    bsz, seq, chans = glu.shape
    width = w_dw.shape[0]
    halo = -(-(width - 1) // V7X_SUBLANES) * V7X_SUBLANES
    tr = _tile(seq, 128)
    assert tr >= halo and chans % V7X_LANES == 0
    row = lambda v: v.reshape(1, chans)
    blocks = 2 * _nbytes((tr, chans), F32) + _nbytes((tr, chans), BF16) + 5 * _nbytes((32, chans), F32)
    taps_at = _conv_taps_per_shift(width)
    sh_shape = (len(taps_at), tr + V7X_SUBLANES * (max(taps_at) - 1), chans)
    scratch = _nbytes((tr + halo, chans), F32) + _nbytes(sh_shape, F32) + _nbytes((tr, chans), F32)
    kernel = functools.partial(_conv_ln_silu_kernel, tr=tr, width=width, halo=halo)
    return pl.pallas_call(
        kernel,
        grid=(bsz, seq // tr),
        in_specs=[pl.BlockSpec((1, tr, chans), lambda b, i: (b, i, 0)),
                  pl.BlockSpec((1, tr, chans), lambda b, i: (b, jnp.maximum(i - 1, 0), 0)),
                  pl.BlockSpec((width, chans), lambda b, i: (0, 0)),
                  pl.BlockSpec((1, chans), lambda b, i: (0, 0)),
                  pl.BlockSpec((1, chans), lambda b, i: (0, 0)),
                  pl.BlockSpec((1, chans), lambda b, i: (0, 0))],
        out_specs=pl.BlockSpec((1, tr, chans), lambda b, i: (b, i, 0)),
        out_shape=jax.ShapeDtypeStruct((bsz, seq, chans), BF16),
        scratch_shapes=[pltpu.VMEM((tr + halo, chans), F32), pltpu.VMEM(sh_shape, F32),
                        pltpu.VMEM((tr, chans), F32)],
        compiler_params=pltpu.CompilerParams(
            dimension_semantics=("parallel", "parallel"),
            vmem_limit_bytes=_vmem_limit(blocks, scratch, 4 * _nbytes((tr, chans), F32))),
        name="conv_ln_silu",
    )(glu, glu, w_dw, row(b_dw), row(ln_g), row(ln_b))


def _two_head_block_diag(slab, half):
    x = slab.astype(F32)
    lo = lax.broadcasted_iota(jnp.int32, x.shape, 1) < HEAD_DIM
    swapped = pltpu.roll(x, HEAD_DIM, 1)
    top = jnp.where(lo, x if half == 0 else swapped, 0.0)
    bot = jnp.where(lo, 0.0, swapped if half == 0 else x)
    return jnp.concatenate([top, bot], axis=0).astype(BF16)


def _attention_kernel(sink_ref, q_ref, kv_ref, o_ref, kcat_ref, vcat_ref, s_ref, p_ref, *, n_heads, n_kv):
    n = pl.program_id(1)
    kv_dim = n_kv * HEAD_DIM
    keys = 2 * WINDOW
    wstart = pl.multiple_of(jnp.maximum(n - 1, 0) * WINDOW, WINDOW)
    qi = lax.broadcasted_iota(jnp.int32, (WINDOW, keys), 0)
    kj = lax.broadcasted_iota(jnp.int32, (WINDOW, keys), 1)
    diff = qi - kj + (n * WINDOW - wstart)
    valid = (diff >= 0) & (diff < WINDOW)
    pairs_per_kv = (n_heads // n_kv) // 2
    for h in range(n_kv):
        slab, half = divmod(h, 2)
        ks = slice(slab * V7X_LANES, (slab + 1) * V7X_LANES)
        vs = slice(kv_dim + slab * V7X_LANES, kv_dim + (slab + 1) * V7X_LANES)
        kcat_ref[h] = _two_head_block_diag(kv_ref[0, pl.ds(wstart, keys), ks], half)
        vcat_ref[h] = _two_head_block_diag(kv_ref[0, pl.ds(wstart, keys), vs], half)
    for pair in range(n_heads // 2):
        cs = slice(pair * V7X_LANES, (pair + 1) * V7X_LANES)
        s = lax.dot_general(q_ref[0, :, cs], kcat_ref[pair // pairs_per_kv], (((1,), (1,)), ((), ())),
                            preferred_element_type=F32)
        s_ref[2 * pair] = s[:, :keys]
        s_ref[2 * pair + 1] = s[:, keys:]
    for head in range(n_heads):
        sh = jnp.where(valid, s_ref[head], MASK_VALUE)
        sink = sink_ref[head]
        mx = jnp.maximum(jnp.max(sh, axis=1, keepdims=True), sink)
        p = jnp.exp(sh - mx)
        den = jnp.sum(p, axis=1, keepdims=True) + jnp.exp(sink - mx)
        p_ref[head] = (p * (1.0 / den)).astype(BF16)
    for pair in range(n_heads // 2):
        cs = slice(pair * V7X_LANES, (pair + 1) * V7X_LANES)
        probs = jnp.concatenate([p_ref[2 * pair], p_ref[2 * pair + 1]], axis=1)
        out = jnp.dot(probs, vcat_ref[pair // pairs_per_kv], preferred_element_type=F32)
        o_ref[0, :, cs] = out.astype(o_ref.dtype)


def _attention(q, kv, sinks, n_heads, n_kv):
    bsz, seq, q_dim = q.shape
    kv_cols = kv.shape[-1]
    assert seq % WINDOW == 0 and seq >= 2 * WINDOW and n_kv % 2 == 0 and (n_heads // n_kv) % 2 == 0
    blocks = 2 * _nbytes((WINDOW, q_dim), BF16) + _nbytes((seq, kv_cols), BF16)
    scratch_shapes = [pltpu.VMEM((n_kv, 4 * WINDOW, V7X_LANES), BF16),
                      pltpu.VMEM((n_kv, 4 * WINDOW, V7X_LANES), BF16),
                      pltpu.VMEM((n_heads, WINDOW, 2 * WINDOW), F32),
                      pltpu.VMEM((n_heads, WINDOW, 2 * WINDOW), BF16)]
    scratch = (2 * _nbytes((n_kv, 4 * WINDOW, V7X_LANES), BF16)
               + _nbytes((n_heads, WINDOW, 2 * WINDOW), F32) + _nbytes((n_heads, WINDOW, 2 * WINDOW), BF16))
    kernel = functools.partial(_attention_kernel, n_heads=n_heads, n_kv=n_kv)
    return pl.pallas_call(
        kernel,
        grid=(bsz, seq // WINDOW),
        in_specs=[pl.BlockSpec(memory_space=pltpu.SMEM),
                  pl.BlockSpec((1, WINDOW, q_dim), lambda b, i: (b, i, 0)),
                  pl.BlockSpec((1, seq, kv_cols), lambda b, i: (b, 0, 0))],
        out_specs=pl.BlockSpec((1, WINDOW, q_dim), lambda b, i: (b, i, 0)),
        out_shape=jax.ShapeDtypeStruct((bsz, seq, q_dim), BF16),
        scratch_shapes=scratch_shapes,
        compiler_params=pltpu.CompilerParams(
            dimension_semantics=("parallel", "parallel"),
            vmem_limit_bytes=_vmem_limit(blocks, scratch, 4 << 20)),
        name="swa_attention",
    )(sinks, q, kv)


def _glu_epilogue(accs, tile, col, row):
    return (accs[0] + col[0]) * jax.nn.sigmoid(accs[1] + col[1])


def _sigmoid_epilogue(accs, tile, col, row):
    return jax.nn.sigmoid(accs[0])


def _gated_merge_epilogue(accs, tile, col, row):
    return tile[0] * accs[0] + tile[1] * accs[1]


def _rownorm_relu2_epilogue(accs, tile, col, row, *, width):
    inv_rms = lax.rsqrt(jnp.sum(row[0], axis=-1, keepdims=True) / width + EPS)
    return jnp.square(jnp.maximum(accs[0] * inv_rms, 0.0))


def kernel(x, norm_mix_g, w_in, b_glu, w_dw, b_dw, conv_ln_g, conv_ln_b, w_conv_out, sinks,
           w_attn_out, w_out, norm_mlp_g, w_mlp_up, w_mlp_down, norm_final_g):
    bsz, seq, d = x.shape
    t = bsz * seq
    n_heads = sinks.shape[0]
    n_kv = n_heads // GROUP
    q_dim, kv_dim = n_heads * HEAD_DIM, n_kv * HEAD_DIM
    conv_dim = w_dw.shape[1]
    off_k, off_v, off_conv = q_dim, q_dim + kv_dim, q_dim + 2 * kv_dim
    off_gc = off_conv + 2 * conv_dim
    assert w_in.shape[1] == off_gc + 2 * d

    w_in_bf = w_in.astype(BF16)

    xf = x.reshape(t, d)
    h = _rmsnorm(xf, norm_mix_g, BF16)

    q = _rope_matmul(h, w_in_bf, 0, q_dim, _rope_tables(seq, HEAD_DIM ** -0.5), seq, q_dim,
                     name="q_proj_rope")
    kv = _rope_matmul(h, w_in_bf, off_k, 2 * kv_dim, _rope_tables(seq, 1.0), seq, kv_dim,
                      name="kv_proj_rope")
    glu, (w_co, w_ao, w_o) = _fused_matmul(
        [h], [(w_in_bf, off_conv), (w_in_bf, off_conv + conv_dim)], [(0, 0), (0, 1)], _glu_epilogue, F32,
        n=conv_dim, col_extras=[b_glu[:conv_dim].reshape(1, conv_dim), b_glu[conv_dim:].reshape(1, conv_dim)],
        side_casts=[w_conv_out, w_attn_out, w_out], tn=512, name="conv_glu_proj")
    gates, (w_up, w_down) = _fused_matmul(
        [h], [(w_in_bf, off_gc)], [(0, 0)], _sigmoid_epilogue, BF16, n=2 * d,
        side_casts=[w_mlp_up, w_mlp_down], tn=512, name="gate_proj")

    z = _conv_ln_silu(glu.reshape(bsz, seq, conv_dim), w_dw, b_dw, conv_ln_g, conv_ln_b)
    ao = _attention(q.reshape(bsz, seq, q_dim), kv.reshape(bsz, seq, 2 * kv_dim), sinks, n_heads, n_kv)

    merged = _fused_matmul([z.reshape(t, conv_dim), ao.reshape(t, q_dim)], [(w_co, 0), (w_ao, 0)],
                           [(0, 0), (1, 1)], _gated_merge_epilogue, BF16, n=d,
                           tile_extras=[(gates, 0), (gates, d)], tn=512, name="branch_merge")
    x1, x1g, ssq = _residual_matmul(merged, w_o, xf, norm_mlp_g, name="mix_out_proj")

    act = _fused_matmul([x1g], [(w_up, 0)], [(0, 0)], functools.partial(_rownorm_relu2_epilogue, width=d),
                        BF16, n=w_up.shape[1], row_extras=[ssq], name="mlp_up")
    x2 = _matmul_kgrid(act, w_down, x1, name="mlp_down")
    y = _rmsnorm(x2, norm_final_g, F32)
    return y.reshape(bsz, seq, d)
```

```python
import functools

import jax
import jax.numpy as jnp
from jax import lax
from jax.experimental import pallas as pl
from jax.experimental.pallas import tpu as pltpu

EPS = 1e-6
HEAD_DIM = 64
GROUP = 8
WINDOW = 128
ROPE_THETA = 10000.0
MASK_VALUE = -1e30

V7X_LANES = 128
V7X_SUBLANES = 8
M_CHUNK = 256
ATTN_ROWS_PER_STEP = 128
V7X_VMEM_BYTES = 64 * 1024 * 1024
VMEM_LIMIT_CAP = V7X_VMEM_BYTES - 6 * 1024 * 1024

F32 = jnp.float32
BF16 = jnp.bfloat16


def _tile(dim, pref, offsets=()):
    if dim <= pref and all(o % dim == 0 for o in offsets):
        return dim
    t = (min(dim, pref) // V7X_LANES) * V7X_LANES
    while t > V7X_LANES and (dim % t or any(o % t for o in offsets)):
        t -= V7X_LANES
    assert dim % t == 0 and all(o % t == 0 for o in offsets), (dim, pref, offsets)
    return t


def _nbytes(shape, dtype):
    n = jnp.dtype(dtype).itemsize
    for s in shape:
        n *= s
    return n


def _vmem_limit(block_bytes, scratch_bytes, temp_bytes):
    est = 2 * block_bytes + scratch_bytes + 2 * temp_bytes + (4 << 20)
    return int(min(max(est, 16 << 20), VMEM_LIMIT_CAP))


def _rmsnorm_kernel(x_ref, g_ref, o_ref):
    x = x_ref[...]
    ms = jnp.mean(x * x, axis=-1, keepdims=True)
    o_ref[...] = (x * lax.rsqrt(ms + EPS) * g_ref[...]).astype(o_ref.dtype)


def _rmsnorm(x, g, out_dtype):
    t, d = x.shape
    tr = _tile(t, 512)
    blocks = _nbytes((tr, d), F32) + _nbytes((tr, d), out_dtype)
    return pl.pallas_call(
        _rmsnorm_kernel,
        grid=(t // tr,),
        in_specs=[pl.BlockSpec((tr, d), lambda i: (i, 0)),
                  pl.BlockSpec((1, d), lambda i: (0, 0))],
        out_specs=pl.BlockSpec((tr, d), lambda i: (i, 0)),
        out_shape=jax.ShapeDtypeStruct((t, d), out_dtype),
        compiler_params=pltpu.CompilerParams(
            dimension_semantics=("parallel",),
            vmem_limit_bytes=_vmem_limit(blocks, 0, 2 * _nbytes((tr, d), F32))),
        name="rmsnorm",
    )(x, g.reshape(1, d))


def _fused_matmul_kernel(*refs, n_lhs, n_rhs, pairs, n_tile, n_col, n_row, n_cast, epilogue):
    lhs = refs[:n_lhs]
    rhs = refs[n_lhs:n_lhs + n_rhs]
    n_in = n_lhs + n_rhs + n_tile + n_col + n_row
    extras = refs[n_lhs + n_rhs:n_in]
    cast_src = refs[n_in:n_in + n_cast]
    o_ref = refs[n_in + n_cast]
    cast_dst = refs[n_in + n_cast + 1:]
    for src, dst in zip(cast_src, cast_dst):
        dst[...] = src[...].astype(dst.dtype)
    tm = o_ref.shape[0]
    step = min(tm, M_CHUNK)
    for c in range(tm // step):
        rs = slice(c * step, (c + 1) * step)
        accs = [jnp.dot(lhs[a][rs, :], rhs[b][...], preferred_element_type=F32) for a, b in pairs]
        out = epilogue(accs, [r[rs, :] for r in extras[:n_tile]],
                       [r[...] for r in extras[n_tile:n_tile + n_col]],
                       [r[rs, :] for r in extras[n_tile + n_col:]])
        o_ref[rs, :] = out.astype(o_ref.dtype)


def _fused_matmul(lhs, rhs, pairs, epilogue, out_dtype, *, n, tile_extras=(), col_extras=(),
                  row_extras=(), side_casts=(), tm=1024, tn=1024, name):
    m = lhs[0].shape[0]
    tm = _tile(m, tm)
    tn = _tile(n, tn, [off for _, off in rhs] + [off for _, off in tile_extras])
    nj = n // tn
    n_steps = (m // tm) * nj
    in_specs, operands, blocks = [], [], 0
    for a in lhs:
        in_specs.append(pl.BlockSpec((tm, a.shape[1]), lambda i, j: (i, 0)))
        operands.append(a)
        blocks += _nbytes((tm, a.shape[1]), a.dtype)
    for b, col_off in rhs:
        assert col_off % tn == 0 and col_off + n <= b.shape[1]
        in_specs.append(pl.BlockSpec((b.shape[0], tn), functools.partial(
            lambda i, j, o: (0, j + o), o=col_off // tn)))
        operands.append(b)
        blocks += _nbytes((b.shape[0], tn), b.dtype)
    for arr, col_off in tile_extras:
        assert col_off % tn == 0
        in_specs.append(pl.BlockSpec((tm, tn), functools.partial(
            lambda i, j, o: (i, j + o), o=col_off // tn)))
        operands.append(arr)
        blocks += _nbytes((tm, tn), arr.dtype)
    for arr in col_extras:
        in_specs.append(pl.BlockSpec((1, tn), lambda i, j: (0, j)))
        operands.append(arr)
        blocks += _nbytes((V7X_SUBLANES, tn), arr.dtype)
    for arr in row_extras:
        in_specs.append(pl.BlockSpec((tm, arr.shape[1]), lambda i, j: (i, 0)))
        operands.append(arr)
        blocks += _nbytes((tm, arr.shape[1]), arr.dtype)
    out_specs = [pl.BlockSpec((tm, tn), lambda i, j: (i, j))]
    out_shape = [jax.ShapeDtypeStruct((m, n), out_dtype)]
    for arr in side_casts:
        rows, cols = arr.shape
        slab = rows // n_steps
        assert slab * n_steps == rows and slab % (2 * V7X_SUBLANES) == 0, (arr.shape, n_steps)
        spec = pl.BlockSpec((slab, cols), lambda i, j: (i * nj + j, 0))
        in_specs.append(spec)
        operands.append(arr)
        out_specs.append(spec)
        out_shape.append(jax.ShapeDtypeStruct(arr.shape, BF16))
        blocks += _nbytes((slab, cols), F32) + _nbytes((slab, cols), BF16)
    blocks += _nbytes((tm, tn), out_dtype)
    temps = (len(pairs) + 1) * _nbytes((tm, tn), F32)
    kernel = functools.partial(
        _fused_matmul_kernel, n_lhs=len(lhs), n_rhs=len(rhs), pairs=tuple(pairs),
        n_tile=len(tile_extras), n_col=len(col_extras), n_row=len(row_extras),
        n_cast=len(side_casts), epilogue=epilogue)
    outs = pl.pallas_call(
        kernel,
        grid=(m // tm, nj),
        in_specs=in_specs,
        out_specs=out_specs,
        out_shape=out_shape,
        compiler_params=pltpu.CompilerParams(
            dimension_semantics=("parallel", "parallel"),
            vmem_limit_bytes=_vmem_limit(blocks, 0, temps)),
        name=name,
    )(*operands)
    return (outs[0], list(outs[1:])) if side_casts else outs[0]


def _matmul_kgrid_kernel(a_ref, b_ref, r_ref, o_ref):
    @pl.when(pl.program_id(2) == 0)
    def _():
        o_ref[...] = r_ref[...]

    tm = o_ref.shape[0]
    step = min(tm, M_CHUNK)
    for r in range(tm // step):
        rs = slice(r * step, (r + 1) * step)
        o_ref[rs, :] += jnp.dot(a_ref[rs, :], b_ref[...], preferred_element_type=F32)


def _matmul_kgrid(a, b, resid, *, tm=1024, tn=1024, tk=4096, name):
    m, kdim = a.shape
    n = b.shape[1]
    tm, tn, tk = _tile(m, tm), _tile(n, tn), _tile(kdim, tk)
    blocks = (_nbytes((tm, tk), a.dtype) + _nbytes((tk, tn), b.dtype)
              + 2 * _nbytes((tm, tn), F32))
    return pl.pallas_call(
        _matmul_kgrid_kernel,
        grid=(m // tm, n // tn, kdim // tk),
        in_specs=[pl.BlockSpec((tm, tk), lambda i, j, k: (i, k)),
                  pl.BlockSpec((tk, tn), lambda i, j, k: (k, j)),
                  pl.BlockSpec((tm, tn), lambda i, j, k: (i, j))],
        out_specs=pl.BlockSpec((tm, tn), lambda i, j, k: (i, j)),
        out_shape=jax.ShapeDtypeStruct((m, n), F32),
        compiler_params=pltpu.CompilerParams(
            dimension_semantics=("parallel", "parallel", "arbitrary"),
            vmem_limit_bytes=_vmem_limit(blocks, 0, 2 * _nbytes((tm, tn), F32))),
        name=name,
    )(a, b, resid)


def _residual_matmul_kernel(a_ref, b_ref, x_ref, g_ref, x1_ref, xg_ref, ssq_ref):
    @pl.when(pl.program_id(1) == 0)
    def _():
        ssq_ref[...] = jnp.zeros_like(ssq_ref)

    tm, tn = x1_ref.shape
    step = min(tm, M_CHUNK)
    for r in range(tm // step):
        rs = slice(r * step, (r + 1) * step)
        x1 = x_ref[rs, :] + jnp.dot(a_ref[rs, :], b_ref[...], preferred_element_type=F32)
        x1_ref[rs, :] = x1
        xg_ref[rs, :] = (x1 * g_ref[...]).astype(xg_ref.dtype)
        sq = x1 * x1
        part = sq[:, :V7X_LANES]
        for c in range(1, tn // V7X_LANES):
            part = part + sq[:, c * V7X_LANES:(c + 1) * V7X_LANES]
        ssq_ref[rs, :] += part


def _residual_matmul(a, b, x, g, *, tm=1024, tn=512, name):
    m, kdim = a.shape
    n = b.shape[1]
    tm, tn = _tile(m, tm), _tile(n, tn)
    blocks = (_nbytes((tm, kdim), a.dtype) + _nbytes((kdim, tn), b.dtype) + 2 * _nbytes((tm, tn), F32)
              + _nbytes((tm, tn), BF16) + _nbytes((tm, V7X_LANES), F32) + _nbytes((V7X_SUBLANES, tn), F32))
    return pl.pallas_call(
        _residual_matmul_kernel,
        grid=(m // tm, n // tn),
        in_specs=[pl.BlockSpec((tm, kdim), lambda i, j: (i, 0)),
                  pl.BlockSpec((kdim, tn), lambda i, j: (0, j)),
                  pl.BlockSpec((tm, tn), lambda i, j: (i, j)),
                  pl.BlockSpec((1, tn), lambda i, j: (0, j))],
        out_specs=[pl.BlockSpec((tm, tn), lambda i, j: (i, j)),
                   pl.BlockSpec((tm, tn), lambda i, j: (i, j)),
                   pl.BlockSpec((tm, V7X_LANES), lambda i, j: (i, 0))],
        out_shape=[jax.ShapeDtypeStruct((m, n), F32),
                   jax.ShapeDtypeStruct((m, n), BF16),
                   jax.ShapeDtypeStruct((m, V7X_LANES), F32)],
        compiler_params=pltpu.CompilerParams(
            dimension_semantics=("parallel", "arbitrary"),
            vmem_limit_bytes=_vmem_limit(blocks, 0, 3 * _nbytes((tm, tn), F32))),
        name=name,
    )(a, b, x, g.reshape(1, n))


def _rope_matmul_kernel(*refs, n_rope_chunks, cast_ranges):
    a_ref, b_ref, cos_ref, sin_lo_ref, sin_hi_ref = refs[:5]
    n_src = 1 if cast_ranges else 0
    o_ref = refs[5 + n_src]
    for (lo, hi), dst in zip(cast_ranges, refs[6 + n_src:]):
        dst[...] = refs[5][:, lo:hi].astype(dst.dtype)
    half = HEAD_DIM // 2
    tm, tn = o_ref.shape
    step = min(tm, M_CHUNK)
    for r in range(tm // step):
        rs = slice(r * step, (r + 1) * step)
        acc = jnp.dot(a_ref[rs, :], b_ref[...], preferred_element_type=F32)
        for c in range(tn // V7X_LANES):
            cs = slice(c * V7X_LANES, (c + 1) * V7X_LANES)
            xc = acc[:, cs]
            if c < n_rope_chunks:
                xc = (xc * cos_ref[rs, :]
                      + pltpu.roll(xc, V7X_LANES - half, 1) * sin_lo_ref[rs, :]
                      + pltpu.roll(xc, half, 1) * sin_hi_ref[rs, :])
            o_ref[rs, cs] = xc.astype(o_ref.dtype)


def _rope_matmul(a, b, col_off, n, tables, seq, n_rope_cols, *, side_split=None, tm=1024, tn=1024, name):
    m, kdim = a.shape
    tm = _tile(seq, tm)
    tn = _tile(n, tn, [col_off])
    assert n_rope_cols in (0, n) or tn == n
    assert col_off % tn == 0 and col_off + n <= b.shape[1]
    col_blk = col_off // tn
    seq_tiles = seq // tm
    nj = n // tn
    tab_spec = pl.BlockSpec((tm, V7X_LANES), lambda i, j: (i % seq_tiles, 0))
    blocks = (_nbytes((tm, kdim), a.dtype) + _nbytes((kdim, tn), b.dtype)
              + 3 * _nbytes((tm, V7X_LANES), F32) + _nbytes((tm, tn), BF16))
    in_specs = [pl.BlockSpec((tm, kdim), lambda i, j: (i, 0)),
                pl.BlockSpec((kdim, tn), lambda i, j: (0, j + col_blk)),
                tab_spec, tab_spec, tab_spec]
    operands = [a, b, *tables]
    out_specs = [pl.BlockSpec((tm, tn), lambda i, j: (i, j))]
    out_shape = [jax.ShapeDtypeStruct((m, n), BF16)]
    cast_ranges = ()
    if side_split is not None:
        w, cast_ranges = side_split[0], tuple(side_split[1])
        rows, cols = w.shape
        n_steps = (m // tm) * nj
        slab = rows // n_steps
        assert slab * n_steps == rows and slab % (2 * V7X_SUBLANES) == 0, (w.shape, n_steps)
        slab_map = lambda i, j: (i * nj + j, 0)
        in_specs.append(pl.BlockSpec((slab, cols), slab_map))
        operands.append(w)
        blocks += _nbytes((slab, cols), F32)
        for lo, hi in cast_ranges:
            assert lo % V7X_LANES == 0 and hi % V7X_LANES == 0 and 0 <= lo < hi <= cols
            out_specs.append(pl.BlockSpec((slab, hi - lo), slab_map))
            out_shape.append(jax.ShapeDtypeStruct((rows, hi - lo), BF16))
            blocks += _nbytes((slab, hi - lo), BF16)
    kernel = functools.partial(_rope_matmul_kernel, n_rope_chunks=min(n_rope_cols, tn) // V7X_LANES,
                               cast_ranges=cast_ranges)
    outs = pl.pallas_call(
        kernel,
        grid=(m // tm, nj),
        in_specs=in_specs,
        out_specs=out_specs,
        out_shape=out_shape,
        compiler_params=pltpu.CompilerParams(
            dimension_semantics=("parallel", "parallel"),
            vmem_limit_bytes=_vmem_limit(blocks, 0, 2 * _nbytes((tm, tn), F32))),
        name=name,
    )(*operands)
    return (outs[0], list(outs[1:])) if side_split is not None else outs[0]


def _rope_tables(seq, scale):
    half = HEAD_DIM // 2
    inv_freq = ROPE_THETA ** (-jnp.arange(0, half, dtype=F32) / half)
    ang = jnp.arange(seq, dtype=jnp.int32).astype(F32)[:, None] * inv_freq[None, :]
    cos = jnp.tile(jnp.cos(ang), (1, V7X_LANES // half)) * scale
    sin = jnp.tile(jnp.sin(ang), (1, V7X_LANES // half)) * scale
    first_half = (jnp.arange(V7X_LANES) % HEAD_DIM) < half
    return cos, jnp.where(first_half, -sin, 0.0), jnp.where(first_half, 0.0, sin)


def _conv_taps_per_shift(width):
    return [-(-(width - r) // V7X_SUBLANES) for r in range(min(V7X_SUBLANES, width))]


def _conv_ln_silu_kernel(cur_ref, prev_ref, wdw_ref, bdw_ref, g_ref, b_ref, o_ref, win_ref, sh_ref, dw_ref,
                         *, tr, width, halo):
    i = pl.program_id(1)
    chans = cur_ref.shape[-1]
    win_ref[0:halo, :] = jnp.where(i > 0, prev_ref[0, tr - halo:tr, :], 0.0)
    win_ref[halo:halo + tr, :] = cur_ref[0]
    lead = halo - (width - 1)
    taps_at = _conv_taps_per_shift(width)
    for c in range(chans // V7X_LANES):
        cs = slice(c * V7X_LANES, (c + 1) * V7X_LANES)
        acc = None
        for r, n_a in enumerate(taps_at):
            rows = tr + V7X_SUBLANES * (n_a - 1)
            sh_ref[r, 0:rows, cs] = win_ref[pl.ds(lead + r, rows), cs]
            xr = sh_ref[r, 0:rows, cs]
            for a in range(n_a):
                tap = V7X_SUBLANES * a + r
                term = xr[V7X_SUBLANES * a:V7X_SUBLANES * a + tr, :] * wdw_ref[tap:tap + 1, cs]
                acc = term if acc is None else acc + term
        dw_ref[:, cs] = acc + bdw_ref[:, cs]
    x = dw_ref[...]
    mu = jnp.mean(x, axis=-1, keepdims=True)
    xc = x - mu
    var = jnp.mean(xc * xc, axis=-1, keepdims=True)
    y = xc * lax.rsqrt(var + EPS) * g_ref[...] + b_ref[...]
    o_ref[0] = (y * jax.nn.sigmoid(y)).astype(o_ref.dtype)


def _conv_ln_silu(glu, w_dw, b_dw, ln_g, ln_b):
    bsz, seq, chans = glu.shape
    width = w_dw.shape[0]
    halo = -(-(width - 1) // V7X_SUBLANES) * V7X_SUBLANES
    tr = _tile(seq, 128)
    assert tr >= halo and chans % V7X_LANES == 0
    row = lambda v: v.reshape(1, chans)
    blocks = 2 * _nbytes((tr, chans), F32) + _nbytes((tr, chans), BF16) + 5 * _nbytes((32, chans), F32)
    taps_at = _conv_taps_per_shift(width)
    sh_shape = (len(taps_at), tr + V7X_SUBLANES * (max(taps_at) - 1), chans)
    scratch = _nbytes((tr + halo, chans), F32) + _nbytes(sh_shape, F32) + _nbytes((tr, chans), F32)
    kernel = functools.partial(_conv_ln_silu_kernel, tr=tr, width=width, halo=halo)
    return pl.pallas_call(
        kernel,
        grid=(bsz, seq // tr),
        in_specs=[pl.BlockSpec((1, tr, chans), lambda b, i: (b, i, 0)),
                  pl.BlockSpec((1, tr, chans), lambda b, i: (b, jnp.maximum(i - 1, 0), 0)),
                  pl.BlockSpec((width, chans), lambda b, i: (0, 0)),
                  pl.BlockSpec((1, chans), lambda b, i: (0, 0)),
                  pl.BlockSpec((1, chans), lambda b, i: (0, 0)),
                  pl.BlockSpec((1, chans), lambda b, i: (0, 0))],
        out_specs=pl.BlockSpec((1, tr, chans), lambda b, i: (b, i, 0)),
        out_shape=jax.ShapeDtypeStruct((bsz, seq, chans), BF16),
        scratch_shapes=[pltpu.VMEM((tr + halo, chans), F32), pltpu.VMEM(sh_shape, F32),
                        pltpu.VMEM((tr, chans), F32)],
        compiler_params=pltpu.CompilerParams(
            dimension_semantics=("parallel", "parallel"),
            vmem_limit_bytes=_vmem_limit(blocks, scratch, 4 * _nbytes((tr, chans), F32))),
        name="conv_ln_silu",
    )(glu, glu, w_dw, row(b_dw), row(ln_g), row(ln_b))


def _two_head_block_diag(slab, half):
    x = slab.astype(F32)
    lo = lax.broadcasted_iota(jnp.int32, x.shape, 1) < HEAD_DIM
    swapped = pltpu.roll(x, HEAD_DIM, 1)
    top = jnp.where(lo, x if half == 0 else swapped, 0.0)
    bot = jnp.where(lo, 0.0, swapped if half == 0 else x)
    return jnp.concatenate([top, bot], axis=0).astype(BF16)


def _attention_kernel(sink_ref, q_ref, kv_ref, o_ref, kcat_ref, vcat_ref, s_ref, p_ref, *, n_heads, n_kv):
    n_blk = q_ref.shape[1] // WINDOW
    kv_dim = n_kv * HEAD_DIM
    keys = 2 * WINDOW
    pairs_per_kv = (n_heads // n_kv) // 2
    qi = lax.broadcasted_iota(jnp.int32, (WINDOW, keys), 0)
    kj = lax.broadcasted_iota(jnp.int32, (WINDOW, keys), 1)
    valid = []
    for blk in range(n_blk):
        n = pl.program_id(1) * n_blk + blk
        wstart = pl.multiple_of(jnp.maximum(n - 1, 0) * WINDOW, WINDOW)
        diff = qi - kj + (n * WINDOW - wstart)
        valid.append((diff >= 0) & (diff < WINDOW))
        for h in range(n_kv):
            slab, half = divmod(h, 2)
            ks = slice(slab * V7X_LANES, (slab + 1) * V7X_LANES)
            vs = slice(kv_dim + slab * V7X_LANES, kv_dim + (slab + 1) * V7X_LANES)
            kcat_ref[blk * n_kv + h] = _two_head_block_diag(kv_ref[0, pl.ds(wstart, keys), ks], half)
            vcat_ref[blk * n_kv + h] = _two_head_block_diag(kv_ref[0, pl.ds(wstart, keys), vs], half)
    for blk in range(n_blk):
        rs = slice(blk * WINDOW, (blk + 1) * WINDOW)
        for pair in range(n_heads // 2):
            cs = slice(pair * V7X_LANES, (pair + 1) * V7X_LANES)
            s = lax.dot_general(q_ref[0, rs, cs], kcat_ref[blk * n_kv + pair // pairs_per_kv],
                                (((1,), (1,)), ((), ())), preferred_element_type=F32)
            s_ref[blk * n_heads + 2 * pair] = s[:, :keys]
            s_ref[blk * n_heads + 2 * pair + 1] = s[:, keys:]
    for blk in range(n_blk):
        for head in range(n_heads):
            sh = jnp.where(valid[blk], s_ref[blk * n_heads + head], MASK_VALUE)
            sink = sink_ref[head]
            mx = jnp.maximum(jnp.max(sh, axis=1, keepdims=True), sink)
            p = jnp.exp(sh - mx)
            den = jnp.sum(p, axis=1, keepdims=True) + jnp.exp(sink - mx)
            p_ref[blk * n_heads + head] = (p * (1.0 / den)).astype(BF16)
    for blk in range(n_blk):
        rs = slice(blk * WINDOW, (blk + 1) * WINDOW)
        for pair in range(n_heads // 2):
            cs = slice(pair * V7X_LANES, (pair + 1) * V7X_LANES)
            probs = jnp.concatenate([p_ref[blk * n_heads + 2 * pair], p_ref[blk * n_heads + 2 * pair + 1]],
                                    axis=1)
            out = jnp.dot(probs, vcat_ref[blk * n_kv + pair // pairs_per_kv], preferred_element_type=F32)
            o_ref[0, rs, cs] = out.astype(o_ref.dtype)


def _attention(q, kv, sinks, n_heads, n_kv):
    bsz, seq, q_dim = q.shape
    kv_cols = kv.shape[-1]
    assert seq % WINDOW == 0 and seq >= 2 * WINDOW and n_kv % 2 == 0 and (n_heads // n_kv) % 2 == 0
    tq = _tile(seq, ATTN_ROWS_PER_STEP)
    n_blk = tq // WINDOW
    blocks = 2 * _nbytes((tq, q_dim), BF16) + _nbytes((seq, kv_cols), BF16)
    scratch_shapes = [pltpu.VMEM((n_blk * n_kv, 4 * WINDOW, V7X_LANES), BF16),
                      pltpu.VMEM((n_blk * n_kv, 4 * WINDOW, V7X_LANES), BF16),
                      pltpu.VMEM((n_blk * n_heads, WINDOW, 2 * WINDOW), F32),
                      pltpu.VMEM((n_blk * n_heads, WINDOW, 2 * WINDOW), BF16)]
    scratch = n_blk * (2 * _nbytes((n_kv, 4 * WINDOW, V7X_LANES), BF16)
                       + _nbytes((n_heads, WINDOW, 2 * WINDOW), F32)
                       + _nbytes((n_heads, WINDOW, 2 * WINDOW), BF16))
    kernel = functools.partial(_attention_kernel, n_heads=n_heads, n_kv=n_kv)
    return pl.pallas_call(
        kernel,
        grid=(bsz, seq // tq),
        in_specs=[pl.BlockSpec(memory_space=pltpu.SMEM),
                  pl.BlockSpec((1, tq, q_dim), lambda b, i: (b, i, 0)),
                  pl.BlockSpec((1, seq, kv_cols), lambda b, i: (b, 0, 0))],
        out_specs=pl.BlockSpec((1, tq, q_dim), lambda b, i: (b, i, 0)),
        out_shape=jax.ShapeDtypeStruct((bsz, seq, q_dim), BF16),
        scratch_shapes=scratch_shapes,
        compiler_params=pltpu.CompilerParams(
            dimension_semantics=("parallel", "parallel"),
            vmem_limit_bytes=_vmem_limit(blocks, scratch, 4 << 20)),
        name="swa_attention",
    )(sinks, q, kv)


def _glu_epilogue(accs, tile, col, row):
    return (accs[0] + col[0]) * jax.nn.sigmoid(accs[1] + col[1])


def _sigmoid_epilogue(accs, tile, col, row):
    return jax.nn.sigmoid(accs[0])


def _gated_merge_epilogue(accs, tile, col, row):
    return tile[0] * accs[0] + tile[1] * accs[1]


def _rownorm_relu2_epilogue(accs, tile, col, row, *, width):
    inv_rms = lax.rsqrt(jnp.sum(row[0], axis=-1, keepdims=True) / width + EPS)
    return jnp.square(jnp.maximum(accs[0] * inv_rms, 0.0))


def kernel(x, norm_mix_g, w_in, b_glu, w_dw, b_dw, conv_ln_g, conv_ln_b, w_conv_out, sinks,
           w_attn_out, w_out, norm_mlp_g, w_mlp_up, w_mlp_down, norm_final_g):
    bsz, seq, d = x.shape
    t = bsz * seq
    n_heads = sinks.shape[0]
    n_kv = n_heads // GROUP
    q_dim, kv_dim = n_heads * HEAD_DIM, n_kv * HEAD_DIM
    conv_dim = w_dw.shape[1]
    off_k, off_v, off_conv = q_dim, q_dim + kv_dim, q_dim + 2 * kv_dim
    off_gc = off_conv + 2 * conv_dim
    assert w_in.shape[1] == off_gc + 2 * d

    w_qkv = w_in[:, :off_conv].astype(BF16)

    xf = x.reshape(t, d)
    h = _rmsnorm(xf, norm_mix_g, BF16)

    q, (w_ag, w_gates) = _rope_matmul(
        h, w_qkv, 0, q_dim, _rope_tables(seq, HEAD_DIM ** -0.5), seq, q_dim,
        side_split=(w_in, [(off_conv, off_gc), (off_gc, off_gc + 2 * d)]), tn=512, name="q_proj_rope")
    kv = _rope_matmul(h, w_qkv, off_k, 2 * kv_dim, _rope_tables(seq, 1.0), seq, kv_dim,
                      name="kv_proj_rope")
    glu, (w_co, w_ao, w_o) = _fused_matmul(
        [h], [(w_ag, 0), (w_ag, conv_dim)], [(0, 0), (0, 1)], _glu_epilogue, F32,
        n=conv_dim, col_extras=[b_glu[:conv_dim].reshape(1, conv_dim), b_glu[conv_dim:].reshape(1, conv_dim)],
        side_casts=[w_conv_out, w_attn_out, w_out], tn=512, name="conv_glu_proj")
    gates, (w_up, w_down) = _fused_matmul(
        [h], [(w_gates, 0)], [(0, 0)], _sigmoid_epilogue, BF16, n=2 * d,
        side_casts=[w_mlp_up, w_mlp_down], name="gate_proj")

    z = _conv_ln_silu(glu.reshape(bsz, seq, conv_dim), w_dw, b_dw, conv_ln_g, conv_ln_b)
    ao = _attention(q.reshape(bsz, seq, q_dim), kv.reshape(bsz, seq, 2 * kv_dim), sinks, n_heads, n_kv)

    merged = _fused_matmul([z.reshape(t, conv_dim), ao.reshape(t, q_dim)], [(w_co, 0), (w_ao, 0)],
                           [(0, 0), (1, 1)], _gated_merge_epilogue, BF16, n=d,
                           tile_extras=[(gates, 0), (gates, d)], tn=512, name="branch_merge")
    x1, x1g, ssq = _residual_matmul(merged, w_o, xf, norm_mlp_g, name="mix_out_proj")

    act = _fused_matmul([x1g], [(w_up, 0)], [(0, 0)], functools.partial(_rownorm_relu2_epilogue, width=d),
                        BF16, n=w_up.shape[1], row_extras=[ssq], name="mlp_up")
    x2 = _matmul_kgrid(act, w_down, x1, name="mlp_down")
    y = _rmsnorm(x2, norm_final_g, F32)
    return y.reshape(bsz, seq, d)
```

```python
import functools

import jax
import jax.numpy as jnp
from jax import lax
from jax.experimental import pallas as pl
from jax.experimental.pallas import tpu as pltpu

EPS = 1e-6
HEAD_DIM = 64
GROUP = 8
WINDOW = 128
ROPE_THETA = 10000.0
MASK_VALUE = -1e30

V7X_LANES = 128
V7X_SUBLANES = 8
M_CHUNK = 256
CONV_TIME_STEPS = 16
ATTN_ROWS_PER_STEP = 128
V7X_VMEM_BYTES = 64 * 1024 * 1024
VMEM_LIMIT_CAP = V7X_VMEM_BYTES - 6 * 1024 * 1024

F32 = jnp.float32
BF16 = jnp.bfloat16


def _tile(dim, pref, offsets=()):
    if dim <= pref and all(o % dim == 0 for o in offsets):
        return dim
    t = (min(dim, pref) // V7X_LANES) * V7X_LANES
    while t > V7X_LANES and (dim % t or any(o % t for o in offsets)):
        t -= V7X_LANES
    assert dim % t == 0 and all(o % t == 0 for o in offsets), (dim, pref, offsets)
    return t


def _nbytes(shape, dtype):
    n = jnp.dtype(dtype).itemsize
    for s in shape:
        n *= s
    return n


def _vmem_limit(block_bytes, scratch_bytes, temp_bytes):
    est = 2 * block_bytes + scratch_bytes + 2 * temp_bytes + (4 << 20)
    return int(min(max(est, 16 << 20), VMEM_LIMIT_CAP))


def _rmsnorm_kernel(x_ref, g_ref, o_ref):
    x = x_ref[...]
    ms = jnp.mean(x * x, axis=-1, keepdims=True)
    o_ref[...] = (x * lax.rsqrt(ms + EPS) * g_ref[...]).astype(o_ref.dtype)


def _rmsnorm(x, g, out_dtype):
    t, d = x.shape
    tr = _tile(t, 512)
    blocks = _nbytes((tr, d), F32) + _nbytes((tr, d), out_dtype)
    return pl.pallas_call(
        _rmsnorm_kernel,
        grid=(t // tr,),
        in_specs=[pl.BlockSpec((tr, d), lambda i: (i, 0)),
                  pl.BlockSpec((1, d), lambda i: (0, 0))],
        out_specs=pl.BlockSpec((tr, d), lambda i: (i, 0)),
        out_shape=jax.ShapeDtypeStruct((t, d), out_dtype),
        compiler_params=pltpu.CompilerParams(
            dimension_semantics=("parallel",),
            vmem_limit_bytes=_vmem_limit(blocks, 0, 2 * _nbytes((tr, d), F32))),
        name="rmsnorm",
    )(x, g.reshape(1, d))


def _fused_matmul_kernel(*refs, n_lhs, n_rhs, pairs, n_tile, n_col, n_row, n_cast, epilogue):
    lhs = refs[:n_lhs]
    rhs = refs[n_lhs:n_lhs + n_rhs]
    n_in = n_lhs + n_rhs + n_tile + n_col + n_row
    extras = refs[n_lhs + n_rhs:n_in]
    cast_src = refs[n_in:n_in + n_cast]
    o_ref = refs[n_in + n_cast]
    cast_dst = refs[n_in + n_cast + 1:]
    for src, dst in zip(cast_src, cast_dst):
        dst[...] = src[...].astype(dst.dtype)
    tm = o_ref.shape[0]
    step = min(tm, M_CHUNK)
    for c in range(tm // step):
        rs = slice(c * step, (c + 1) * step)
        accs = [jnp.dot(lhs[a][rs, :], rhs[b][...], preferred_element_type=F32) for a, b in pairs]
        out = epilogue(accs, [r[rs, :] for r in extras[:n_tile]],
                       [r[...] for r in extras[n_tile:n_tile + n_col]],
                       [r[rs, :] for r in extras[n_tile + n_col:]])
        o_ref[rs, :] = out.astype(o_ref.dtype)


def _fused_matmul(lhs, rhs, pairs, epilogue, out_dtype, *, n, tile_extras=(), col_extras=(),
                  row_extras=(), side_casts=(), tm=1024, tn=1024, name):
    m = lhs[0].shape[0]
    tm = _tile(m, tm)
    tn = _tile(n, tn, [off for _, off in rhs] + [off for _, off in tile_extras])
    nj = n // tn
    n_steps = (m // tm) * nj
    in_specs, operands, blocks = [], [], 0
    for a in lhs:
        in_specs.append(pl.BlockSpec((tm, a.shape[1]), lambda i, j: (i, 0)))
        operands.append(a)
        blocks += _nbytes((tm, a.shape[1]), a.dtype)
    for b, col_off in rhs:
        assert col_off % tn == 0 and col_off + n <= b.shape[1]
        in_specs.append(pl.BlockSpec((b.shape[0], tn), functools.partial(
            lambda i, j, o: (0, j + o), o=col_off // tn)))
        operands.append(b)
        blocks += _nbytes((b.shape[0], tn), b.dtype)
    for arr, col_off in tile_extras:
        assert col_off % tn == 0
        in_specs.append(pl.BlockSpec((tm, tn), functools.partial(
            lambda i, j, o: (i, j + o), o=col_off // tn)))
        operands.append(arr)
        blocks += _nbytes((tm, tn), arr.dtype)
    for arr in col_extras:
        in_specs.append(pl.BlockSpec((1, tn), lambda i, j: (0, j)))
        operands.append(arr)
        blocks += _nbytes((V7X_SUBLANES, tn), arr.dtype)
    for arr in row_extras:
        in_specs.append(pl.BlockSpec((tm, arr.shape[1]), lambda i, j: (i, 0)))
        operands.append(arr)
        blocks += _nbytes((tm, arr.shape[1]), arr.dtype)
    out_specs = [pl.BlockSpec((tm, tn), lambda i, j: (i, j))]
    out_shape = [jax.ShapeDtypeStruct((m, n), out_dtype)]
    for arr in side_casts:
        rows, cols = arr.shape
        slab = rows // n_steps
        assert slab * n_steps == rows and slab % (2 * V7X_SUBLANES) == 0, (arr.shape, n_steps)
        spec = pl.BlockSpec((slab, cols), lambda i, j: (i * nj + j, 0))
        in_specs.append(spec)
        operands.append(arr)
        out_specs.append(spec)
        out_shape.append(jax.ShapeDtypeStruct(arr.shape, BF16))
        blocks += _nbytes((slab, cols), F32) + _nbytes((slab, cols), BF16)
    blocks += _nbytes((tm, tn), out_dtype)
    temps = (len(pairs) + 1) * _nbytes((tm, tn), F32)
    kernel = functools.partial(
        _fused_matmul_kernel, n_lhs=len(lhs), n_rhs=len(rhs), pairs=tuple(pairs),
        n_tile=len(tile_extras), n_col=len(col_extras), n_row=len(row_extras),
        n_cast=len(side_casts), epilogue=epilogue)
    outs = pl.pallas_call(
        kernel,
        grid=(m // tm, nj),
        in_specs=in_specs,
        out_specs=out_specs,
        out_shape=out_shape,
        compiler_params=pltpu.CompilerParams(
            dimension_semantics=("parallel", "parallel"),
            vmem_limit_bytes=_vmem_limit(blocks, 0, temps)),
        name=name,
    )(*operands)
    return (outs[0], list(outs[1:])) if side_casts else outs[0]


def _matmul_kgrid_kernel(a_ref, b_ref, r_ref, o_ref):
    @pl.when(pl.program_id(2) == 0)
    def _():
        o_ref[...] = r_ref[...]

    tm = o_ref.shape[0]
    step = min(tm, M_CHUNK)
    for r in range(tm // step):
        rs = slice(r * step, (r + 1) * step)
        o_ref[rs, :] += jnp.dot(a_ref[rs, :], b_ref[...], preferred_element_type=F32)


def _matmul_kgrid(a, b, resid, *, tm=1024, tn=1024, tk=4096, name):
    m, kdim = a.shape
    n = b.shape[1]
    tm, tn, tk = _tile(m, tm), _tile(n, tn), _tile(kdim, tk)
    blocks = (_nbytes((tm, tk), a.dtype) + _nbytes((tk, tn), b.dtype)
              + 2 * _nbytes((tm, tn), F32))
    return pl.pallas_call(
        _matmul_kgrid_kernel,
        grid=(m // tm, n // tn, kdim // tk),
        in_specs=[pl.BlockSpec((tm, tk), lambda i, j, k: (i, k)),
                  pl.BlockSpec((tk, tn), lambda i, j, k: (k, j)),
                  pl.BlockSpec((tm, tn), lambda i, j, k: (i, j))],
        out_specs=pl.BlockSpec((tm, tn), lambda i, j, k: (i, j)),
        out_shape=jax.ShapeDtypeStruct((m, n), F32),
        compiler_params=pltpu.CompilerParams(
            dimension_semantics=("parallel", "parallel", "arbitrary"),
            vmem_limit_bytes=_vmem_limit(blocks, 0, 2 * _nbytes((tm, tn), F32))),
        name=name,
    )(a, b, resid)


def _residual_matmul_kernel(a_ref, b_ref, x_ref, g_ref, x1_ref, xg_ref, ssq_ref):
    @pl.when(pl.program_id(1) == 0)
    def _():
        ssq_ref[...] = jnp.zeros_like(ssq_ref)

    tm, tn = x1_ref.shape
    step = min(tm, M_CHUNK)
    for r in range(tm // step):
        rs = slice(r * step, (r + 1) * step)
        x1 = x_ref[rs, :] + jnp.dot(a_ref[rs, :], b_ref[...], preferred_element_type=F32)
        x1_ref[rs, :] = x1
        xg_ref[rs, :] = (x1 * g_ref[...]).astype(xg_ref.dtype)
        sq = x1 * x1
        part = sq[:, :V7X_LANES]
        for c in range(1, tn // V7X_LANES):
            part = part + sq[:, c * V7X_LANES:(c + 1) * V7X_LANES]
        ssq_ref[rs, :] += part


def _residual_matmul(a, b, x, g, *, tm=1024, tn=512, name):
    m, kdim = a.shape
    n = b.shape[1]
    tm, tn = _tile(m, tm), _tile(n, tn)
    blocks = (_nbytes((tm, kdim), a.dtype) + _nbytes((kdim, tn), b.dtype) + 2 * _nbytes((tm, tn), F32)
              + _nbytes((tm, tn), BF16) + _nbytes((tm, V7X_LANES), F32) + _nbytes((V7X_SUBLANES, tn), F32))
    return pl.pallas_call(
        _residual_matmul_kernel,
        grid=(m // tm, n // tn),
        in_specs=[pl.BlockSpec((tm, kdim), lambda i, j: (i, 0)),
                  pl.BlockSpec((kdim, tn), lambda i, j: (0, j)),
                  pl.BlockSpec((tm, tn), lambda i, j: (i, j)),
                  pl.BlockSpec((1, tn), lambda i, j: (0, j))],
        out_specs=[pl.BlockSpec((tm, tn), lambda i, j: (i, j)),
                   pl.BlockSpec((tm, tn), lambda i, j: (i, j)),
                   pl.BlockSpec((tm, V7X_LANES), lambda i, j: (i, 0))],
        out_shape=[jax.ShapeDtypeStruct((m, n), F32),
                   jax.ShapeDtypeStruct((m, n), BF16),
                   jax.ShapeDtypeStruct((m, V7X_LANES), F32)],
        compiler_params=pltpu.CompilerParams(
            dimension_semantics=("parallel", "arbitrary"),
            vmem_limit_bytes=_vmem_limit(blocks, 0, 3 * _nbytes((tm, tn), F32))),
        name=name,
    )(a, b, x, g.reshape(1, n))


def _rope_matmul_kernel(*refs, n_rope_chunks, cast_ranges):
    a_ref, b_ref, cos_ref, sin_lo_ref, sin_hi_ref = refs[:5]
    n_src = 1 if cast_ranges else 0
    o_ref = refs[5 + n_src]
    for (lo, hi), dst in zip(cast_ranges, refs[6 + n_src:]):
        dst[...] = refs[5][:, lo:hi].astype(dst.dtype)
    half = HEAD_DIM // 2
    tm, tn = o_ref.shape
    step = min(tm, M_CHUNK)
    for r in range(tm // step):
        rs = slice(r * step, (r + 1) * step)
        acc = jnp.dot(a_ref[rs, :], b_ref[...], preferred_element_type=F32)
        for c in range(tn // V7X_LANES):
            cs = slice(c * V7X_LANES, (c + 1) * V7X_LANES)
            xc = acc[:, cs]
            if c < n_rope_chunks:
                xc = (xc * cos_ref[rs, :]
                      + pltpu.roll(xc, V7X_LANES - half, 1) * sin_lo_ref[rs, :]
                      + pltpu.roll(xc, half, 1) * sin_hi_ref[rs, :])
            o_ref[rs, cs] = xc.astype(o_ref.dtype)


def _rope_matmul(a, b, col_off, n, tables, seq, n_rope_cols, *, side_split=None, tm=1024, tn=1024, name):
    m, kdim = a.shape
    tm = _tile(seq, tm)
    tn = _tile(n, tn, [col_off])
    assert n_rope_cols in (0, n) or tn == n
    assert col_off % tn == 0 and col_off + n <= b.shape[1]
    col_blk = col_off // tn
    seq_tiles = seq // tm
    nj = n // tn
    tab_spec = pl.BlockSpec((tm, V7X_LANES), lambda i, j: (i % seq_tiles, 0))
    blocks = (_nbytes((tm, kdim), a.dtype) + _nbytes((kdim, tn), b.dtype)
              + 3 * _nbytes((tm, V7X_LANES), F32) + _nbytes((tm, tn), BF16))
    in_specs = [pl.BlockSpec((tm, kdim), lambda i, j: (i, 0)),
                pl.BlockSpec((kdim, tn), lambda i, j: (0, j + col_blk)),
                tab_spec, tab_spec, tab_spec]
    operands = [a, b, *tables]
    out_specs = [pl.BlockSpec((tm, tn), lambda i, j: (i, j))]
    out_shape = [jax.ShapeDtypeStruct((m, n), BF16)]
    cast_ranges = ()
    if side_split is not None:
        w, cast_ranges = side_split[0], tuple(side_split[1])
        rows, cols = w.shape
        n_steps = (m // tm) * nj
        slab = rows // n_steps
        assert slab * n_steps == rows and slab % (2 * V7X_SUBLANES) == 0, (w.shape, n_steps)
        slab_map = lambda i, j: (i * nj + j, 0)
        in_specs.append(pl.BlockSpec((slab, cols), slab_map))
        operands.append(w)
        blocks += _nbytes((slab, cols), F32)
        for lo, hi in cast_ranges:
            assert lo % V7X_LANES == 0 and hi % V7X_LANES == 0 and 0 <= lo < hi <= cols
            out_specs.append(pl.BlockSpec((slab, hi - lo), slab_map))
            out_shape.append(jax.ShapeDtypeStruct((rows, hi - lo), BF16))
            blocks += _nbytes((slab, hi - lo), BF16)
    kernel = functools.partial(_rope_matmul_kernel, n_rope_chunks=min(n_rope_cols, tn) // V7X_LANES,
                               cast_ranges=cast_ranges)
    outs = pl.pallas_call(
        kernel,
        grid=(m // tm, nj),
        in_specs=in_specs,
        out_specs=out_specs,
        out_shape=out_shape,
        compiler_params=pltpu.CompilerParams(
            dimension_semantics=("parallel", "parallel"),
            vmem_limit_bytes=_vmem_limit(blocks, 0, 2 * _nbytes((tm, tn), F32))),
        name=name,
    )(*operands)
    return (outs[0], list(outs[1:])) if side_split is not None else outs[0]


def _rope_tables(seq, scale):
    half = HEAD_DIM // 2
    inv_freq = ROPE_THETA ** (-jnp.arange(0, half, dtype=F32) / half)
    ang = jnp.arange(seq, dtype=jnp.int32).astype(F32)[:, None] * inv_freq[None, :]
    cos = jnp.tile(jnp.cos(ang), (1, V7X_LANES // half)) * scale
    sin = jnp.tile(jnp.sin(ang), (1, V7X_LANES // half)) * scale
    first_half = (jnp.arange(V7X_LANES) % HEAD_DIM) < half
    return cos, jnp.where(first_half, -sin, 0.0), jnp.where(first_half, 0.0, sin)


def _conv_taps_per_shift(width):
    return [-(-(width - r) // V7X_SUBLANES) for r in range(min(V7X_SUBLANES, width))]


def _conv_ln_silu_kernel(cur_ref, prev_ref, wdw_ref, bdw_ref, g_ref, b_ref, o_ref, win_ref, sh_ref, dw_ref,
                         *, tr, width, halo):
    i = pl.program_id(1)
    chans = cur_ref.shape[-1]
    win_ref[0:halo, :] = jnp.where(i > 0, prev_ref[0, tr - halo:tr, :], 0.0)
    win_ref[halo:halo + tr, :] = cur_ref[0]
    lead = halo - (width - 1)
    nc = chans // V7X_LANES
    w3 = wdw_ref[...].reshape(width, nc, V7X_LANES)
    b3 = bdw_ref[...].reshape(1, nc, V7X_LANES)
    sh_ref[...] = win_ref[...].reshape(halo + tr, nc, V7X_LANES)
    for t0 in range(0, tr, CONV_TIME_STEPS):
        acc = None
        for tap in range(width):
            term = sh_ref[t0 + lead + tap:t0 + lead + tap + CONV_TIME_STEPS] * w3[tap:tap + 1]
            acc = term if acc is None else acc + term
        dw_ref[t0:t0 + CONV_TIME_STEPS, :] = (acc + b3).reshape(CONV_TIME_STEPS, chans)
    x = dw_ref[...]
    mu = jnp.mean(x, axis=-1, keepdims=True)
    xc = x - mu
    var = jnp.mean(xc * xc, axis=-1, keepdims=True)
    y = xc * lax.rsqrt(var + EPS) * g_ref[...] + b_ref[...]
    o_ref[0] = (y * jax.nn.sigmoid(y)).astype(o_ref.dtype)


def _conv_ln_silu(glu, w_dw, b_dw, ln_g, ln_b):
    bsz, seq, chans = glu.shape
    width = w_dw.shape[0]
    halo = -(-(width - 1) // V7X_SUBLANES) * V7X_SUBLANES
    tr = _tile(seq, 128)
    assert tr >= halo and chans % V7X_LANES == 0
    row = lambda v: v.reshape(1, chans)
    blocks = 2 * _nbytes((tr, chans), F32) + _nbytes((tr, chans), BF16) + 5 * _nbytes((32, chans), F32)
    taps_at = _conv_taps_per_shift(width)
    sh_shape = (halo + tr, chans // V7X_LANES, V7X_LANES)
    scratch = _nbytes((tr + halo, chans), F32) + _nbytes(sh_shape, F32) + _nbytes((tr, chans), F32)
    kernel = functools.partial(_conv_ln_silu_kernel, tr=tr, width=width, halo=halo)
    return pl.pallas_call(
        kernel,
        grid=(bsz, seq // tr),
        in_specs=[pl.BlockSpec((1, tr, chans), lambda b, i: (b, i, 0)),
                  pl.BlockSpec((1, tr, chans), lambda b, i: (b, jnp.maximum(i - 1, 0), 0)),
                  pl.BlockSpec((width, chans), lambda b, i: (0, 0)),
                  pl.BlockSpec((1, chans), lambda b, i: (0, 0)),
                  pl.BlockSpec((1, chans), lambda b, i: (0, 0)),
                  pl.BlockSpec((1, chans), lambda b, i: (0, 0))],
        out_specs=pl.BlockSpec((1, tr, chans), lambda b, i: (b, i, 0)),
        out_shape=jax.ShapeDtypeStruct((bsz, seq, chans), BF16),
        scratch_shapes=[pltpu.VMEM((tr + halo, chans), F32), pltpu.VMEM(sh_shape, F32),
                        pltpu.VMEM((tr, chans), F32)],
        compiler_params=pltpu.CompilerParams(
            dimension_semantics=("parallel", "parallel"),
            vmem_limit_bytes=_vmem_limit(blocks, scratch, 4 * _nbytes((tr, chans), F32))),
        name="conv_ln_silu",
    )(glu, glu, w_dw, row(b_dw), row(ln_g), row(ln_b))


def _two_head_block_diag(slab, half):
    x = slab.astype(F32)
    lo = lax.broadcasted_iota(jnp.int32, x.shape, 1) < HEAD_DIM
    swapped = pltpu.roll(x, HEAD_DIM, 1)
    top = jnp.where(lo, x if half == 0 else swapped, 0.0)
    bot = jnp.where(lo, 0.0, swapped if half == 0 else x)
    return jnp.concatenate([top, bot], axis=0).astype(BF16)


def _attention_kernel(sink_ref, q_ref, kv_ref, o_ref, kcat_ref, vcat_ref, s_ref, p_ref, *, n_heads, n_kv):
    n_blk = q_ref.shape[1] // WINDOW
    kv_dim = n_kv * HEAD_DIM
    keys = 2 * WINDOW
    pairs_per_kv = (n_heads // n_kv) // 2
    qi = lax.broadcasted_iota(jnp.int32, (WINDOW, keys), 0)
    kj = lax.broadcasted_iota(jnp.int32, (WINDOW, keys), 1)
    valid = []
    for blk in range(n_blk):
        n = pl.program_id(1) * n_blk + blk
        wstart = pl.multiple_of(jnp.maximum(n - 1, 0) * WINDOW, WINDOW)
        diff = qi - kj + (n * WINDOW - wstart)
        valid.append((diff >= 0) & (diff < WINDOW))
        for h in range(n_kv):
            slab, half = divmod(h, 2)
            ks = slice(slab * V7X_LANES, (slab + 1) * V7X_LANES)
            vs = slice(kv_dim + slab * V7X_LANES, kv_dim + (slab + 1) * V7X_LANES)
            kcat_ref[blk * n_kv + h] = _two_head_block_diag(kv_ref[0, pl.ds(wstart, keys), ks], half)
            vcat_ref[blk * n_kv + h] = _two_head_block_diag(kv_ref[0, pl.ds(wstart, keys), vs], half)
    for blk in range(n_blk):
        rs = slice(blk * WINDOW, (blk + 1) * WINDOW)
        for pair in range(n_heads // 2):
            cs = slice(pair * V7X_LANES, (pair + 1) * V7X_LANES)
            s = lax.dot_general(q_ref[0, rs, cs], kcat_ref[blk * n_kv + pair // pairs_per_kv],
                                (((1,), (1,)), ((), ())), preferred_element_type=F32)
            s_ref[blk * n_heads + 2 * pair] = s[:, :keys]
            s_ref[blk * n_heads + 2 * pair + 1] = s[:, keys:]
    for blk in range(n_blk):
        for head in range(n_heads):
            sh = jnp.where(valid[blk], s_ref[blk * n_heads + head], MASK_VALUE)
            sink = sink_ref[head]
            mx = jnp.maximum(jnp.max(sh, axis=1, keepdims=True), sink)
            p = jnp.exp(sh - mx)
            den = jnp.sum(p, axis=1, keepdims=True) + jnp.exp(sink - mx)
            p_ref[blk * n_heads + head] = (p * (1.0 / den)).astype(BF16)
    for blk in range(n_blk):
        rs = slice(blk * WINDOW, (blk + 1) * WINDOW)
        for pair in range(n_heads // 2):
            cs = slice(pair * V7X_LANES, (pair + 1) * V7X_LANES)
            probs = jnp.concatenate([p_ref[blk * n_heads + 2 * pair], p_ref[blk * n_heads + 2 * pair + 1]],
                                    axis=1)
            out = jnp.dot(probs, vcat_ref[blk * n_kv + pair // pairs_per_kv], preferred_element_type=F32)
            o_ref[0, rs, cs] = out.astype(o_ref.dtype)


def _attention(q, kv, sinks, n_heads, n_kv):
    bsz, seq, q_dim = q.shape
    kv_cols = kv.shape[-1]
    assert seq % WINDOW == 0 and seq >= 2 * WINDOW and n_kv % 2 == 0 and (n_heads // n_kv) % 2 == 0
    tq = _tile(seq, ATTN_ROWS_PER_STEP)
    n_blk = tq // WINDOW
    blocks = 2 * _nbytes((tq, q_dim), BF16) + _nbytes((seq, kv_cols), BF16)
    scratch_shapes = [pltpu.VMEM((n_blk * n_kv, 4 * WINDOW, V7X_LANES), BF16),
                      pltpu.VMEM((n_blk * n_kv, 4 * WINDOW, V7X_LANES), BF16),
                      pltpu.VMEM((n_blk * n_heads, WINDOW, 2 * WINDOW), F32),
                      pltpu.VMEM((n_blk * n_heads, WINDOW, 2 * WINDOW), BF16)]
    scratch = n_blk * (2 * _nbytes((n_kv, 4 * WINDOW, V7X_LANES), BF16)
                       + _nbytes((n_heads, WINDOW, 2 * WINDOW), F32)
                       + _nbytes((n_heads, WINDOW, 2 * WINDOW), BF16))
    kernel = functools.partial(_attention_kernel, n_heads=n_heads, n_kv=n_kv)
    return pl.pallas_call(
        kernel,
        grid=(bsz, seq // tq),
        in_specs=[pl.BlockSpec(memory_space=pltpu.SMEM),
                  pl.BlockSpec((1, tq, q_dim), lambda b, i: (b, i, 0)),
                  pl.BlockSpec((1, seq, kv_cols), lambda b, i: (b, 0, 0))],
        out_specs=pl.BlockSpec((1, tq, q_dim), lambda b, i: (b, i, 0)),
        out_shape=jax.ShapeDtypeStruct((bsz, seq, q_dim), BF16),
        scratch_shapes=scratch_shapes,
        compiler_params=pltpu.CompilerParams(
            dimension_semantics=("parallel", "parallel"),
            vmem_limit_bytes=_vmem_limit(blocks, scratch, 4 << 20)),
        name="swa_attention",
    )(sinks, q, kv)


def _glu_epilogue(accs, tile, col, row):
    return (accs[0] + col[0]) * jax.nn.sigmoid(accs[1] + col[1])


def _sigmoid_epilogue(accs, tile, col, row):
    return jax.nn.sigmoid(accs[0])


def _gated_merge_epilogue(accs, tile, col, row):
    return tile[0] * accs[0] + tile[1] * accs[1]


def _rownorm_relu2_epilogue(accs, tile, col, row, *, width):
    inv_rms = lax.rsqrt(jnp.sum(row[0], axis=-1, keepdims=True) / width + EPS)
    return jnp.square(jnp.maximum(accs[0] * inv_rms, 0.0))


def kernel(x, norm_mix_g, w_in, b_glu, w_dw, b_dw, conv_ln_g, conv_ln_b, w_conv_out, sinks,
           w_attn_out, w_out, norm_mlp_g, w_mlp_up, w_mlp_down, norm_final_g):
    bsz, seq, d = x.shape
    t = bsz * seq
    n_heads = sinks.shape[0]
    n_kv = n_heads // GROUP
    q_dim, kv_dim = n_heads * HEAD_DIM, n_kv * HEAD_DIM
    conv_dim = w_dw.shape[1]
    off_k, off_v, off_conv = q_dim, q_dim + kv_dim, q_dim + 2 * kv_dim
    off_gc = off_conv + 2 * conv_dim
    assert w_in.shape[1] == off_gc + 2 * d

    w_qkv = w_in[:, :off_conv].astype(BF16)

    xf = x.reshape(t, d)
    h = _rmsnorm(xf, norm_mix_g, BF16)

    q, (w_ag, w_gates) = _rope_matmul(
        h, w_qkv, 0, q_dim, _rope_tables(seq, HEAD_DIM ** -0.5), seq, q_dim,
        side_split=(w_in, [(off_conv, off_gc), (off_gc, off_gc + 2 * d)]), tn=512, name="q_proj_rope")
    kv = _rope_matmul(h, w_qkv, off_k, 2 * kv_dim, _rope_tables(seq, 1.0), seq, kv_dim,
                      name="kv_proj_rope")
    glu, (w_co, w_ao, w_o) = _fused_matmul(
        [h], [(w_ag, 0), (w_ag, conv_dim)], [(0, 0), (0, 1)], _glu_epilogue, F32,
        n=conv_dim, col_extras=[b_glu[:conv_dim].reshape(1, conv_dim), b_glu[conv_dim:].reshape(1, conv_dim)],
        side_casts=[w_conv_out, w_attn_out, w_out], tn=512, name="conv_glu_proj")
    gates, (w_up, w_down) = _fused_matmul(
        [h], [(w_gates, 0)], [(0, 0)], _sigmoid_epilogue, BF16, n=2 * d,
        side_casts=[w_mlp_up, w_mlp_down], name="gate_proj")

    z = _conv_ln_silu(glu.reshape(bsz, seq, conv_dim), w_dw, b_dw, conv_ln_g, conv_ln_b)
    ao = _attention(q.reshape(bsz, seq, q_dim), kv.reshape(bsz, seq, 2 * kv_dim), sinks, n_heads, n_kv)

    merged = _fused_matmul([z.reshape(t, conv_dim), ao.reshape(t, q_dim)], [(w_co, 0), (w_ao, 0)],
                           [(0, 0), (1, 1)], _gated_merge_epilogue, BF16, n=d,
                           tile_extras=[(gates, 0), (gates, d)], tn=512, name="branch_merge")
    x1, x1g, ssq = _residual_matmul(merged, w_o, xf, norm_mlp_g, name="mix_out_proj")

    act = _fused_matmul([x1g], [(w_up, 0)], [(0, 0)], functools.partial(_rownorm_relu2_epilogue, width=d),
                        BF16, n=w_up.shape[1], row_extras=[ssq], name="mlp_up")
    x2 = _matmul_kgrid(act, w_down, x1, name="mlp_down")
    y = _rmsnorm(x2, norm_final_g, F32)
    return y.reshape(bsz, seq, d)
```
